```python
import math
import jax
import jax.numpy as jnp
from jax import lax
import numpy as np

D_MODEL = 1024
BATCH = 8
SEQ = 8192
DEPTH = 2

GRID_W = 64
CTX_LEN = 256
N_EVEN = (DEPTH + 1) // 2
N_ODD = DEPTH // 2
EPS = 1e-6

MIX_W = D_MODEL
HY_W = MIX_W // 2
HY_ORDER = 2
HY_SHORT = 3
HY_BANDS = 16
HY_EMB = 1 + 2 * HY_BANDS
HY_FILTER_HID = 64
HY_MOD_SHIFT = 0.05
HY_FAST_DECAY = 0.3
HY_SLOW_DECAY = 1.5
HY_DECAY_TARGET = 1e-2
LRU_W = MIX_W - HY_W
LRU_HEADS = 8
LRU_BW = LRU_W // LRU_HEADS
LRU_CONV = 4
LRU_C = 8.0
IN_W = (HY_ORDER + 1) * HY_W + 2 * LRU_W

HEAD_DIM = 64
N_HEADS = D_MODEL // HEAD_DIM
N_KV = 4
GROUP = N_HEADS // N_KV
WINDOW = 128
BLOCK = 128
ROPE_BASE = 10000.0
QKV_W = (N_HEADS + 2 * N_KV) * HEAD_DIM
NEG_INF = -1e30

D_FF = 256 * (-(-(8 * D_MODEL) // (3 * 256)))

kernel_name = 'hybrid_hyena_rglru_swa_prefix_block'


def rms_norm(x, g):
    xf = x.astype(jnp.float32)
    y = xf * lax.rsqrt(jnp.mean(xf * xf, axis=-1, keepdims=True) + EPS)
    return (y * g.astype(jnp.float32)).astype(x.dtype)


def dw_conv(x, w, b):
    k = w.shape[0]
    pl, pr = k // 2, k - 1 - k // 2
    y = lax.conv_general_dilated(x, w[:, None, :], window_strides=(1,), padding=[(pl, pr)],
                                 dimension_numbers=('NWC', 'WIO', 'NWC'),
                                 feature_group_count=x.shape[-1])
    return y + b


def swiglu(h, w1, w3, w2):
    return (jax.nn.silu(h @ w1) * (h @ w3)) @ w2


def hyena_filters(L, w1, b1, w2, b2, w3, freq):
    f32 = jnp.float32
    t = jnp.linspace(0.0, 1.0, L, dtype=f32)[:, None]
    bands = jnp.linspace(1e-4, HY_BANDS - 1, HY_BANDS, dtype=f32)
    w = 2.0 * math.pi * jnp.arange(L, dtype=f32)[:, None] / L
    z = jnp.concatenate([t, jnp.cos(bands * w), -jnp.sin(bands * w)], axis=-1)
    fr = freq.astype(f32)
    h = jnp.sin(fr * (z @ w1.astype(f32) + b1.astype(f32)))
    h = jnp.sin(fr * (h @ w2.astype(f32) + b2.astype(f32)))
    h = (h @ w3.astype(f32)).reshape(L, 2, HY_ORDER, HY_W)
    max_decay = math.log(HY_DECAY_TARGET) / HY_FAST_DECAY
    min_decay = math.log(HY_DECAY_TARGET) / HY_SLOW_DECAY
    deltas = jnp.abs(jnp.linspace(min_decay, max_decay, HY_W, dtype=f32))
    window = jnp.exp(-t * deltas) + HY_MOD_SHIFT
    h = h * window[:, None, None, :]
    fwd = h[:, 0]
    bwd = h[1:, 1]
    k = jnp.concatenate([fwd, jnp.zeros((1, HY_ORDER, HY_W), f32), bwd[::-1]], axis=0)
    return k / jnp.sum(jnp.abs(k), axis=0, keepdims=True)


def long_conv(u, kf):
    L = u.shape[1]
    U = jnp.fft.rfft(u, n=2 * L, axis=1)
    return jnp.fft.irfft(U * kf, n=2 * L, axis=1)[:, :L]


def hyena(u, conv_w, conv_b, filt, bias):
    L = u.shape[1]
    uc = dw_conv(u, conv_w, conv_b).astype(jnp.float32)
    v, x1, x2 = jnp.split(uc, HY_ORDER + 1, axis=-1)
    kf = jnp.fft.rfft(hyena_filters(L, *filt), axis=0)
    bias = bias.astype(jnp.float32)
    z = v
    for n, gate in enumerate((x1, x2)):
        z = gate * (long_conv(z, kf[:, n]) + bias[n] * z)
    return z.astype(u.dtype)


def rglru_coeffs(x, w_a, b_a, w_i, b_i, lam):
    xh = x.reshape(x.shape[:-1] + (LRU_HEADS, LRU_BW))
    r = jax.nn.sigmoid(jnp.einsum('blhi,hij->blhj', xh, w_a).reshape(x.shape) + b_a)
    i = jax.nn.sigmoid(jnp.einsum('blhi,hij->blhj', xh, w_i).reshape(x.shape) + b_i)
    log_a = -LRU_C * r * jax.nn.softplus(-lam)
    a = jnp.exp(log_a)
    b = jnp.sqrt(-jnp.expm1(2.0 * log_a)) * (i * x)
    return a, b


def linear_scan(a, b, h0, reverse):
    if h0 is not None:
        idx = -1 if reverse else 0
        b = b.at[:, idx].add(a[:, idx] * h0)

    def combine(e1, e2):
        a1, b1 = e1
        a2, b2 = e2
        return a1 * a2, a2 * b1 + b2

    _, h = lax.associative_scan(combine, (a, b), reverse=reverse, axis=1)
    return h


def hyena_lru_mixer(hx, hc, w_in, hy_conv_w, hy_conv_b, filt, hy_bias,
                    lru_conv_w, lru_conv_b, lru_params, w_out, ctx_out):
    f32 = jnp.float32
    s_hy = (HY_ORDER + 1) * HY_W
    s_lru = s_hy + LRU_W
    px = hx @ w_in
    hy_x, lx, gx = px[..., :s_hy], px[..., s_hy:s_lru], px[..., s_lru:]
    if ctx_out:
        pc = hc @ w_in
        hy_c, lc, gc = pc[..., :s_hy], pc[..., s_hy:s_lru], pc[..., s_lru:]
    else:
        lc = hc @ w_in[:, s_hy:s_lru]
    lx = dw_conv(lx, lru_conv_w, lru_conv_b).astype(f32)
    lc = dw_conv(lc, lru_conv_w, lru_conv_b).astype(f32)
    rx = 0.0
    rc = 0.0
    for d, reverse in enumerate((False, True)):
        prm = tuple(p[d].astype(f32) for p in lru_params)
        a, b = rglru_coeffs(lc, *prm)
        hcs = linear_scan(a, b, None, reverse)
        h_end = hcs[:, 0] if reverse else hcs[:, -1]
        a, b = rglru_coeffs(lx, *prm)
        rx = rx + linear_scan(a, b, h_end, reverse)
        if ctx_out:
            rc = rc + hcs
    rec_x = (rx * jax.nn.gelu(gx.astype(f32))).astype(hx.dtype)
    out_x = jnp.concatenate([hyena(hy_x, hy_conv_w, hy_conv_b, filt, hy_bias), rec_x], axis=-1) @ w_out
    if not ctx_out:
        return out_x, None
    rec_c = (rc * jax.nn.gelu(gc.astype(f32))).astype(hc.dtype)
    out_c = jnp.concatenate([hyena(hy_c, hy_conv_w, hy_conv_b, filt, hy_bias), rec_c], axis=-1) @ w_out
    return out_x, out_c


def rope_1d(x, pos):
    nf = x.shape[-1] // 2
    inv = jnp.power(ROPE_BASE, -jnp.arange(nf, dtype=jnp.float32) / nf)
    ang = pos.astype(jnp.float32)[:, None] * inv
    cos = jnp.cos(ang)[None, :, None, :]
    sin = jnp.sin(ang)[None, :, None, :]
    xf = x.astype(jnp.float32)
    x1, x2 = xf[..., :nf], xf[..., nf:]
    return jnp.concatenate([x1 * cos - x2 * sin, x2 * cos + x1 * sin], axis=-1).astype(x.dtype)


def rope_2d(x, row, col):
    half = HEAD_DIM // 2
    return jnp.concatenate([rope_1d(x[..., :half], row), rope_1d(x[..., half:], col)], axis=-1)


def window_attention(hx, hc, w_qkv, q_gain, k_gain, sink, w_o, ctx_out):
    f32 = jnp.float32
    B_, S, _ = hx.shape
    C = hc.shape[1]
    nq = N_HEADS * HEAD_DIM
    nkv = N_KV * HEAD_DIM
    scale = HEAD_DIM ** -0.5

    def heads(t, n):
        return t.reshape(t.shape[:-1] + (n, HEAD_DIM))

    qkv_l = hx @ w_qkv
    ql = rms_norm(heads(qkv_l[..., :nq], N_HEADS), q_gain)
    kl = rms_norm(heads(qkv_l[..., nq:nq + nkv], N_KV), k_gain)
    vl = heads(qkv_l[..., nq + nkv:], N_KV)
    if ctx_out:
        qkv_c = hc @ w_qkv
        qc = rms_norm(heads(qkv_c[..., :nq], N_HEADS), q_gain)
        kvc = qkv_c[..., nq:]
    else:
        kvc = hc @ w_qkv[:, nq:]
    kc = rms_norm(heads(kvc[..., :nkv], N_KV), k_gain)
    vc = heads(kvc[..., nkv:], N_KV)

    rows = S // GRID_W
    row = jnp.repeat(jnp.arange(rows, dtype=jnp.int32), GRID_W)
    col = jnp.tile(jnp.arange(GRID_W, dtype=jnp.int32), rows)
    ql = rope_2d(ql, row, col)
    kl = rope_2d(kl, row, col)

    sink_b = sink.astype(f32).reshape(N_KV, GROUP)[None, :, :, None, None]
    qg = ql.reshape(B_, S, N_KV, GROUP, HEAD_DIM)
    kp = jnp.pad(kl, ((0, 0), (BLOCK, BLOCK), (0, 0), (0, 0)))
    vp = jnp.pad(vl, ((0, 0), (BLOCK, BLOCK), (0, 0), (0, 0)))

    def block(n):
        start = n * BLOCK
        qb = lax.dynamic_slice_in_dim(qg, start, BLOCK, axis=1)
        kb = lax.dynamic_slice_in_dim(kp, start, 3 * BLOCK, axis=1)
        vb = lax.dynamic_slice_in_dim(vp, start, 3 * BLOCK, axis=1)
        s_loc = jnp.einsum('bqkgd,bskd->bkgqs', qb, kb).astype(f32) * scale
        s_ctx = jnp.einsum('bqkgd,bckd->bkgqc', qb, kc).astype(f32) * scale
        qpos = start + jnp.arange(BLOCK)
        kpos = start - BLOCK + jnp.arange(3 * BLOCK)
        valid = (jnp.abs(qpos[:, None] - kpos[None, :]) <= WINDOW) & (kpos >= 0) & (kpos < S)
        s_loc = jnp.where(valid, s_loc, NEG_INF)
        s_sink = jnp.broadcast_to(sink_b, s_ctx.shape[:-1] + (1,))
        p = jax.nn.softmax(jnp.concatenate([s_sink, s_ctx, s_loc], axis=-1), axis=-1)
        p_ctx = p[..., 1:1 + C].astype(vc.dtype)
        p_loc = p[..., 1 + C:].astype(vb.dtype)
        return (jnp.einsum('bkgqc,bckd->bqkgd', p_ctx, vc)
                + jnp.einsum('bkgqs,bskd->bqkgd', p_loc, vb))

    o = lax.map(block, jnp.arange(S // BLOCK))
    out_x = jnp.moveaxis(o, 0, 1).reshape(B_, S, nq) @ w_o
    if not ctx_out:
        return out_x, None
    qcg = qc.reshape(B_, C, N_KV, GROUP, HEAD_DIM)
    s = jnp.einsum('bqkgd,bckd->bkgqc', qcg, kc).astype(f32) * scale
    s = jnp.concatenate([jnp.broadcast_to(sink_b, s.shape[:-1] + (1,)), s], axis=-1)
    p = jax.nn.softmax(s, axis=-1)[..., 1:].astype(vc.dtype)
    out_c = jnp.einsum('bkgqc,bckd->bqkgd', p, vc).reshape(B_, C, nq) @ w_o
    return out_x, out_c


def setup_inputs(seed: int = 0) -> dict:
    key = jax.random.key(seed)
    ks = iter(jax.random.split(key, 64))
    f32 = jnp.float32
    D = D_MODEL

    def nrm(shape, scale):
        return jax.random.normal(next(ks), shape, f32) * scale

    def gain(shape):
        return 1.0 + nrm(shape, 0.02)

    a_c = jax.random.uniform(next(ks), (N_EVEN, 2, LRU_W), f32, 0.9, 0.999)
    a = a_c ** (1.0 / LRU_C)
    lam = jnp.log(a) - jnp.log1p(-a)

    return {
        'x': nrm((BATCH, SEQ, D), 1.0),
        'c': nrm((BATCH, D), 1.0),
        'ctx': nrm((BATCH, CTX_LEN, D), 1.0),
        'c_ctx': nrm((D,), 1.0),
        'norm1': gain((DEPTH, D)),
        'norm2': gain((DEPTH, D)),
        'w_mod': nrm((DEPTH, D, 6 * D), 0.5 * D ** -0.5),
        'b_mod': nrm((DEPTH, 6 * D), 0.02),
        'ffn_w1': nrm((DEPTH, D, D_FF), D ** -0.5),
        'ffn_w3': nrm((DEPTH, D, D_FF), D ** -0.5),
        'ffn_w2': nrm((DEPTH, D_FF, D), D_FF ** -0.5),
        'ab_w_in': nrm((N_EVEN, D, IN_W), D ** -0.5),
        'hy_conv_w': nrm((N_EVEN, HY_SHORT, (HY_ORDER + 1) * HY_W), 0.5),
        'hy_conv_b': nrm((N_EVEN, (HY_ORDER + 1) * HY_W), 0.02),
        'hy_f_w1': nrm((N_EVEN, HY_EMB, HY_FILTER_HID), HY_EMB ** -0.5),
        'hy_f_b1': nrm((N_EVEN, HY_FILTER_HID), 0.02),
        'hy_f_w2': nrm((N_EVEN, HY_FILTER_HID, HY_FILTER_HID), HY_FILTER_HID ** -0.5),
        'hy_f_b2': nrm((N_EVEN, HY_FILTER_HID), 0.02),
        'hy_f_w3': nrm((N_EVEN, HY_FILTER_HID, 2 * HY_ORDER * HY_W), HY_FILTER_HID ** -0.5),
        'hy_f_freq': gain((N_EVEN, HY_FILTER_HID)),
        'hy_bias': nrm((N_EVEN, HY_ORDER, HY_W), 0.5),
        'lru_conv_w': nrm((N_EVEN, LRU_CONV, LRU_W), 0.5),
        'lru_conv_b': nrm((N_EVEN, LRU_W), 0.02),
        'lru_w_a': nrm((N_EVEN, 2, LRU_HEADS, LRU_BW, LRU_BW), LRU_BW ** -0.5),
        'lru_b_a': nrm((N_EVEN, 2, LRU_W), 0.02),
        'lru_w_i': nrm((N_EVEN, 2, LRU_HEADS, LRU_BW, LRU_BW), LRU_BW ** -0.5),
        'lru_b_i': nrm((N_EVEN, 2, LRU_W), 0.02),
        'lru_lam': lam,
        'ab_w_out': nrm((N_EVEN, MIX_W, D), MIX_W ** -0.5),
        'at_w_qkv': nrm((N_ODD, D, QKV_W), D ** -0.5),
        'at_q_gain': gain((N_ODD, HEAD_DIM)),
        'at_k_gain': gain((N_ODD, HEAD_DIM)),
        'at_sink': nrm((N_ODD, N_HEADS), 0.5),
        'at_w_o': nrm((N_ODD, N_HEADS * HEAD_DIM, D), (N_HEADS * HEAD_DIM) ** -0.5),
    }


def reference(x, c, ctx, c_ctx, norm1, norm2, w_mod, b_mod, ffn_w1, ffn_w3, ffn_w2,
              ab_w_in, hy_conv_w, hy_conv_b, hy_f_w1, hy_f_b1, hy_f_w2, hy_f_b2, hy_f_w3,
              hy_f_freq, hy_bias, lru_conv_w, lru_conv_b, lru_w_a, lru_b_a, lru_w_i, lru_b_i,
              lru_lam, ab_w_out, at_w_qkv, at_q_gain, at_k_gain, at_sink, at_w_o):
    for i in range(DEPTH):
        last = i == DEPTH - 1
        j = i // 2
        mod_x = (jax.nn.silu(c) @ w_mod[i] + b_mod[i])[:, None, :]
        mod_c = (jax.nn.silu(c_ctx) @ w_mod[i] + b_mod[i])[None, None, :]
        sh1x, sc1x, g1x, sh2x, sc2x, g2x = jnp.split(mod_x, 6, axis=-1)
        sh1c, sc1c, g1c, sh2c, sc2c, g2c = jnp.split(mod_c, 6, axis=-1)
        hx = rms_norm(x, norm1[i]) * (1.0 + sc1x) + sh1x
        hc = rms_norm(ctx, norm1[i]) * (1.0 + sc1c) + sh1c
        if i % 2 == 0:
            filt = (hy_f_w1[j], hy_f_b1[j], hy_f_w2[j], hy_f_b2[j], hy_f_w3[j], hy_f_freq[j])
            lru_params = (lru_w_a[j], lru_b_a[j], lru_w_i[j], lru_b_i[j], lru_lam[j])
            mx, mc = hyena_lru_mixer(hx, hc, ab_w_in[j], hy_conv_w[j], hy_conv_b[j], filt, hy_bias[j],
                                     lru_conv_w[j], lru_conv_b[j], lru_params, ab_w_out[j], not last)
        else:
            mx, mc = window_attention(hx, hc, at_w_qkv[j], at_q_gain[j], at_k_gain[j], at_sink[j],
                                      at_w_o[j], not last)
        x = x + g1x * mx
        x = x + g2x * swiglu(rms_norm(x, norm2[i]) * (1.0 + sc2x) + sh2x, ffn_w1[i], ffn_w3[i], ffn_w2[i])
        if not last:
            ctx = ctx + g1c * mc
            ctx = ctx + g2c * swiglu(rms_norm(ctx, norm2[i]) * (1.0 + sc2c) + sh2c,
                                     ffn_w1[i], ffn_w3[i], ffn_w2[i])
    return x
```

```python
import functools
import math

import numpy as np
import jax
import jax.numpy as jnp
from jax import lax
from jax.experimental import pallas as pl
from jax.experimental.pallas import tpu as pltpu

F32 = jnp.float32
BF16 = jnp.bfloat16

EPS = 1e-6
D_MODEL = 1024
GRID_W = 64
HY_W = 512
HY_ORDER = 2
HY_BANDS = 16
HY_MOD_SHIFT = 0.05
HY_FAST_DECAY = 0.3
HY_SLOW_DECAY = 1.5
HY_DECAY_TARGET = 1e-2
LRU_W = 512
LRU_HEADS = 8
LRU_C = 8.0
HEAD_DIM = 64
N_HEADS = 16
N_KV = 4
GROUP = N_HEADS // N_KV
WINDOW = 128
ROPE_BASE = 10000.0
NEG_INF = -1e30

VMEM_LIMIT_BYTES = 58 * 1024 * 1024
LANES = 128

FFT_R = 128
FFT_H = FFT_R // 2
FFT_K1 = FFT_R // 2 + 1
FFT_K1P = 72
U_PITCH = FFT_R + 8
A_PITCH = 2 * FFT_R + 8


def _cparams(sem):
    return pltpu.CompilerParams(dimension_semantics=sem, vmem_limit_bytes=VMEM_LIMIT_BYTES)


def _resident(block_shape, index_map):
    return pl.BlockSpec(block_shape, index_map, pipeline_mode=pl.Buffered(1))


def _norm_mod(x, gain, scale, shift):
    y = x * lax.rsqrt(jnp.mean(x * x, axis=-1, keepdims=True) + EPS)
    return (y * gain) * (1.0 + scale) + shift


def _proj_kernel(x_ref, gain_ref, sc_ref, sh_ref, w_ref, o_ref, *, n_chunk):
    h = _norm_mod(x_ref[...], gain_ref[...], sc_ref[...], sh_ref[...]).astype(BF16)
    n_out = o_ref.shape[-1]
    for n0 in range(0, n_out, n_chunk):
        o_ref[:, n0:n0 + n_chunk] = jnp.dot(
            h, w_ref[:, n0:n0 + n_chunk], preferred_element_type=F32).astype(o_ref.dtype)


def _proj(x, gain, sc, sh, w, tm):
    b, s, d = x.shape
    n = w.shape[1]
    return pl.pallas_call(
        functools.partial(_proj_kernel, n_chunk=512),
        grid=(b, s // tm),
        in_specs=[
            pl.BlockSpec((None, tm, d), lambda i, j: (i, j, 0)),
            pl.BlockSpec((1, d), lambda i, j: (0, 0)),
            pl.BlockSpec((None, 1, d), lambda i, j: (i, 0, 0)),
            pl.BlockSpec((None, 1, d), lambda i, j: (i, 0, 0)),
            _resident((d, n), lambda i, j: (0, 0)),
        ],
        out_specs=pl.BlockSpec((None, tm, n), lambda i, j: (i, j, 0)),
        out_shape=jax.ShapeDtypeStruct((b, s, n), BF16),
        compiler_params=_cparams(("parallel", "parallel")),
        name="proj_in",
    )(x, gain, sc, sh, w)


def _qkv_kernel(x_ref, gain_ref, sc_ref, sh_ref, w_ref, qg_ref, kg_ref, cos_ref, sin_ref,
                q_ref, k_ref, v_ref):
    tm = x_ref.shape[0]
    nq = q_ref.shape[-1]
    nkv = k_ref.shape[-1]
    h = _norm_mod(x_ref[...], gain_ref[...], sc_ref[...], sh_ref[...]).astype(BF16)
    cos = cos_ref[...]
    sin = sin_ref[...]
    lane = lax.broadcasted_iota(jnp.int32, (tm, LANES), 1)
    low_head = lane < HEAD_DIM
    first_half = (lane % 32) < 16

    def norm_rope(y, g, scale):
        sq = y * y
        s_lo = jnp.sum(jnp.where(low_head, sq, 0.0), axis=-1, keepdims=True)
        s_hi = jnp.sum(jnp.where(low_head, 0.0, sq), axis=-1, keepdims=True)
        inv = lax.rsqrt(jnp.where(low_head, s_lo, s_hi) * (1.0 / HEAD_DIM) + EPS)
        y = (y * inv) * g
        partner = jnp.where(first_half, pltpu.roll(y, LANES - 16, 1), pltpu.roll(y, 16, 1))
        return (y * cos + partner * sin) * scale

    qg = qg_ref[...]
    kg = kg_ref[...]
    for n0 in range(0, nq, 512):
        y = jnp.dot(h, w_ref[:, n0:n0 + 512], preferred_element_type=F32)
        for j in range(4):
            q_ref[:, n0 + j * LANES:n0 + (j + 1) * LANES] = norm_rope(
                y[:, j * LANES:(j + 1) * LANES], qg, HEAD_DIM ** -0.5).astype(BF16)
    y = jnp.dot(h, w_ref[:, nq:nq + nkv], preferred_element_type=F32)
    for j in range(nkv // LANES):
        k_ref[:, j * LANES:(j + 1) * LANES] = norm_rope(
            y[:, j * LANES:(j + 1) * LANES], kg, 1.0).astype(BF16)
    v_ref[...] = jnp.dot(h, w_ref[:, nq + nkv:], preferred_element_type=F32).astype(BF16)


def _qkv(x, gain, sc, sh, w, qg, kg, cos, sin, tm):
    b, s, d = x.shape
    nq = N_HEADS * HEAD_DIM
    nkv = N_KV * HEAD_DIM
    return pl.pallas_call(
        _qkv_kernel,
        grid=(b, s // tm),
        in_specs=[
            pl.BlockSpec((None, tm, d), lambda i, j: (i, j, 0)),
            pl.BlockSpec((1, d), lambda i, j: (0, 0)),
            pl.BlockSpec((None, 1, d), lambda i, j: (i, 0, 0)),
            pl.BlockSpec((None, 1, d), lambda i, j: (i, 0, 0)),
            _resident((d, nq + 2 * nkv), lambda i, j: (0, 0)),
            pl.BlockSpec((1, LANES), lambda i, j: (0, 0)),
            pl.BlockSpec((1, LANES), lambda i, j: (0, 0)),
            pl.BlockSpec((tm, LANES), lambda i, j: (j, 0)),
            pl.BlockSpec((tm, LANES), lambda i, j: (j, 0)),
        ],
        out_specs=[
            pl.BlockSpec((None, tm, nq), lambda i, j: (i, j, 0)),
            pl.BlockSpec((None, tm, nkv), lambda i, j: (i, j, 0)),
            pl.BlockSpec((None, tm, nkv), lambda i, j: (i, j, 0)),
        ],
        out_shape=[
            jax.ShapeDtypeStruct((b, s, nq), BF16),
            jax.ShapeDtypeStruct((b, s, nkv), BF16),
            jax.ShapeDtypeStruct((b, s, nkv), BF16),
        ],
        compiler_params=_cparams(("parallel", "parallel")),
        name="qkv",
    )(x, gain, sc, sh, w, qg, kg, cos, sin)


def _attn_kernel(sink_ref, q_ref, kp_ref, kc_ref, kn_ref, vp_ref, vc_ref, vn_ref,
                 kx_ref, vx_ref, o_ref, *, seq):
    blk = q_ref.shape[0]
    start = pl.program_id(1) * blk
    q = q_ref[...]
    k_all = jnp.concatenate([kp_ref[...], kc_ref[...], kn_ref[...]], axis=0)
    v_all = jnp.concatenate([vp_ref[...], vc_ref[...], vn_ref[...]], axis=0)
    kx = kx_ref[...]
    vx = vx_ref[...]
    rows = GROUP * blk
    r = lax.broadcasted_iota(jnp.int32, (rows, 3 * blk), 0) % blk
    c = lax.broadcasted_iota(jnp.int32, (rows, 3 * blk), 1)
    kpos = start - blk + c
    valid = (jnp.abs(c - blk - r) <= WINDOW) & (kpos >= 0) & (kpos < seq)
    nt = (((1,), (1,)), ((), ()))
    for g in range(N_KV):
        q4 = jnp.concatenate(
            [q[:, (GROUP * g + j) * HEAD_DIM:(GROUP * g + j + 1) * HEAD_DIM] for j in range(GROUP)], axis=0)
        kg = k_all[:, g * HEAD_DIM:(g + 1) * HEAD_DIM]
        vg = v_all[:, g * HEAD_DIM:(g + 1) * HEAD_DIM]
        kxg = kx[:, g * HEAD_DIM:(g + 1) * HEAD_DIM]
        vxg = vx[:, g * HEAD_DIM:(g + 1) * HEAD_DIM]
        s_loc = lax.dot_general(q4, kg, nt, preferred_element_type=F32)
        s_loc = jnp.where(valid, s_loc, NEG_INF)
        s_ctx = lax.dot_general(q4, kxg, nt, preferred_element_type=F32)
        s_sink = jnp.concatenate(
            [jnp.full((blk, 1), sink_ref[GROUP * g + j], F32) for j in range(GROUP)], axis=0)
        m = jnp.maximum(jnp.maximum(jnp.max(s_loc, axis=-1, keepdims=True),
                                    jnp.max(s_ctx, axis=-1, keepdims=True)), s_sink)
        p_loc = jnp.exp(s_loc - m)
        p_ctx = jnp.exp(s_ctx - m)
        denom = (jnp.sum(p_loc, axis=-1, keepdims=True) + jnp.sum(p_ctx, axis=-1, keepdims=True)
                 + jnp.exp(s_sink - m))
        o4 = (jnp.dot(p_loc.astype(BF16), vg, preferred_element_type=F32)
              + jnp.dot(p_ctx.astype(BF16), vxg, preferred_element_type=F32)) / denom
        for j in range(GROUP):
            hd = GROUP * g + j
            o_ref[:, hd * HEAD_DIM:(hd + 1) * HEAD_DIM] = o4[j * blk:(j + 1) * blk].astype(BF16)


def _attention(q, k, v, kx, vx, sink, blk):
    b, s, nq = q.shape
    nkv = k.shape[-1]
    cx = kx.shape[1]
    nb = s // blk
    qspec = pl.BlockSpec((None, blk, nq), lambda i, j, *_: (i, j, 0))
    prev = pl.BlockSpec((None, blk, nkv), lambda i, j, *_: (i, jnp.maximum(j - 1, 0), 0))
    cur = pl.BlockSpec((None, blk, nkv), lambda i, j, *_: (i, j, 0))
    nxt = pl.BlockSpec((None, blk, nkv), lambda i, j, *_: (i, jnp.minimum(j + 1, nb - 1), 0))
    cxs = pl.BlockSpec((None, cx, nkv), lambda i, j, *_: (i, 0, 0))
    return pl.pallas_call(
        functools.partial(_attn_kernel, seq=s),
        grid_spec=pltpu.PrefetchScalarGridSpec(
            num_scalar_prefetch=1,
            grid=(b, nb),
            in_specs=[qspec, prev, cur, nxt, prev, cur, nxt, cxs, cxs],
            out_specs=pl.BlockSpec((None, blk, nq), lambda i, j, *_: (i, j, 0)),
        ),
        out_shape=jax.ShapeDtypeStruct((b, s, nq), BF16),
        compiler_params=_cparams(("parallel", "parallel")),
        name="window_attn",
    )(sink, q, k, k, k, v, v, v, kx, vx)


def _lru_kernel(prev_ref, cur_ref, next_ref, cw_ref, cb_ref, wg_ref, bg_ref, lam_ref, h0_ref,
                h_ref, hend_ref, a_scr, b_scr, carry, *, reverse, n_chunks):
    nb, t, w = cur_ref.shape
    pitch = t + 8
    n_lane_groups = w // LANES
    i = pl.program_id(0)
    chunk = (n_chunks - 1 - i) if reverse else i

    @pl.when(i == 0)
    def _():
        carry[...] = h0_ref[...]

    lam = lam_ref[...]
    softplus_neg_lam = jnp.maximum(-lam, 0.0) + jnp.log1p(jnp.exp(-jnp.abs(lam)))
    cw = cw_ref[...]
    cb = cb_ref[...]
    bg = bg_ref[...]
    row = lax.broadcasted_iota(jnp.int32, (t, w), 0)
    has_prev = chunk > 0
    has_next = chunk < n_chunks - 1
    for b in range(nb):
        cur = cur_ref[b].astype(F32)
        pv = prev_ref[b].astype(F32)
        nx = next_ref[b].astype(F32)
        pm2 = jnp.where(has_prev, pv[14:15], 0.0)
        pm1 = jnp.where(has_prev, pv[15:16], 0.0)
        nx0 = jnp.where(has_next, nx[0:1], 0.0)
        xm1 = jnp.where(row == 0, pm1, pltpu.roll(cur, 1, 0))
        xm2 = jnp.where(row == 0, pm2, jnp.where(row == 1, pm1, pltpu.roll(cur, 2, 0)))
        xp1 = jnp.where(row == t - 1, nx0, pltpu.roll(cur, t - 1, 0))
        x = cw[0:1] * xm2 + cw[1:2] * xm1 + cw[2:3] * cur + cw[3:4] * xp1 + cb
        g = jnp.dot(x.astype(BF16), wg_ref[...], preferred_element_type=F32) + bg
        r = jax.nn.sigmoid(g[:, :w])
        gate_i = jax.nn.sigmoid(g[:, w:])
        log_a = (-LRU_C) * r * softplus_neg_lam
        a = jnp.exp(log_a)
        bb = jnp.sqrt(1.0 - jnp.exp(2.0 * log_a)) * (gate_i * x)
        for gi in range(n_lane_groups):
            a_scr[gi, pl.ds(b * pitch, t), :] = a[:, gi * LANES:(gi + 1) * LANES]
            b_scr[gi, pl.ds(b * pitch, t), :] = bb[:, gi * LANES:(gi + 1) * LANES]

    def step(s, hs):
        tt = (t - 1 - s) if reverse else s
        out = []
        for gi in range(n_lane_groups):
            a_t = a_scr[gi, pl.ds(tt, nb, stride=pitch), :]
            b_t = b_scr[gi, pl.ds(tt, nb, stride=pitch), :]
            h_new = a_t * hs[gi] + b_t
            b_scr[gi, pl.ds(tt, nb, stride=pitch), :] = h_new
            out.append(h_new)
        return tuple(out)

    h_init = tuple(carry[:, gi * LANES:(gi + 1) * LANES] for gi in range(n_lane_groups))
    h_fin = lax.fori_loop(0, t, step, h_init, unroll=8)
    for gi in range(n_lane_groups):
        carry[:, gi * LANES:(gi + 1) * LANES] = h_fin[gi]
        hend_ref[:, gi * LANES:(gi + 1) * LANES] = h_fin[gi]
    for b in range(nb):
        for gi in range(n_lane_groups):
            h_ref[b, :, gi * LANES:(gi + 1) * LANES] = b_scr[gi, pl.ds(b * pitch, t), :]


def _lru(p, col_block, cw, cb, wg, bg, lam, h0, reverse, t):
    nb, s, _ = p.shape
    w = LRU_W
    n_chunks = s // t
    halo = 16
    tb = t // halo
    last_halo = s // halo - 1
    if reverse:
        cidx = lambda i: n_chunks - 1 - i
    else:
        cidx = lambda i: i
    kern = functools.partial(_lru_kernel, reverse=reverse, n_chunks=n_chunks)
    return pl.pallas_call(
        kern,
        grid=(n_chunks,),
        in_specs=[
            pl.BlockSpec((nb, halo, w), lambda i: (0, jnp.maximum(cidx(i) * tb - 1, 0), col_block)),
            pl.BlockSpec((nb, t, w), lambda i: (0, cidx(i), col_block)),
            pl.BlockSpec((nb, halo, w), lambda i: (0, jnp.minimum((cidx(i) + 1) * tb, last_halo), col_block)),
            pl.BlockSpec((4, w), lambda i: (0, 0)),
            pl.BlockSpec((1, w), lambda i: (0, 0)),
            _resident((w, 2 * w), lambda i: (0, 0)),
            pl.BlockSpec((1, 2 * w), lambda i: (0, 0)),
            pl.BlockSpec((1, w), lambda i: (0, 0)),
            pl.BlockSpec((nb, w), lambda i: (0, 0)),
        ],
        out_specs=[
            pl.BlockSpec((nb, t, w), lambda i: (0, cidx(i), 0)),
            pl.BlockSpec((nb, w), lambda i: (0, 0)),
        ],
        out_shape=[
            jax.ShapeDtypeStruct((nb, s, w), F32),
            jax.ShapeDtypeStruct((nb, w), F32),
        ],
        scratch_shapes=[
            pltpu.VMEM((w // LANES, nb * (t + 8), LANES), F32),
            pltpu.VMEM((w // LANES, nb * (t + 8), LANES), F32),
            pltpu.VMEM((nb, w), F32),
        ],
        compiler_params=_cparams(("arbitrary",)),
        name="rglru_rev" if reverse else "rglru_fwd",
    )(p, p, p, cw, cb, wg, bg, lam, h0)


def _conv3_chunk(ref, n1, n_chunks, w, b):
    rows = FFT_R
    base = pl.multiple_of(n1 * rows, rows)
    cur = ref[pl.ds(base, rows), :].astype(F32)
    pbase = pl.multiple_of(jnp.maximum(base - 16, 0), 16)
    nbase = pl.multiple_of(jnp.minimum(base + rows, (n_chunks - 1) * rows), 16)
    prev = jnp.where(n1 > 0, ref[pl.ds(pbase, 16), :].astype(F32)[15:16], 0.0)
    nxt = jnp.where(n1 < n_chunks - 1, ref[pl.ds(nbase, 16), :].astype(F32)[0:1], 0.0)
    row = lax.broadcasted_iota(jnp.int32, cur.shape, 0)
    xm1 = jnp.where(row == 0, prev, pltpu.roll(cur, 1, 0))
    xp1 = jnp.where(row == rows - 1, nxt, pltpu.roll(cur, rows - 1, 0))
    return w[0:1] * xm1 + w[1:2] * cur + w[2:3] * xp1 + b


def _hyena_kernel(v_ref, x1_ref, x2_ref, cwv_ref, cw1_ref, cw2_ref, cbv_ref, cb1_ref, cb2_ref,
                  f1_ref, g_ref, f1i_ref, kf_ref, o_ref, u_scr, a_scr):
    n_chunks = FFT_H
    cwv, cbv = cwv_ref[...], cbv_ref[...]

    def load_v(n1, c):
        u_scr[pl.ds(pl.multiple_of(n1 * U_PITCH, 8), FFT_R), :] = _conv3_chunk(v_ref, n1, n_chunks, cwv, cbv)
        return c

    lax.fori_loop(0, n_chunks, load_v, 0)
    f1 = f1_ref[...]
    f1i = f1i_ref[...]
    tn = (((0,), (0,)), ((), ()))

    for order, (gate_ref, cw_ref, cb_ref) in enumerate(((x1_ref, cw1_ref, cb1_ref), (x2_ref, cw2_ref, cb2_ref))):
        def fwd1(n2, c):
            xs = u_scr[pl.ds(n2, FFT_H, stride=U_PITCH), :].astype(BF16)
            a = jnp.dot(f1, xs, preferred_element_type=F32)
            a_scr[pl.ds(n2, FFT_K1P, stride=A_PITCH), :] = a[:FFT_K1P]
            a_scr[pl.ds(FFT_R + n2, FFT_K1P, stride=A_PITCH), :] = a[FFT_K1P:]
            return c

        lax.fori_loop(0, FFT_R, fwd1, 0)

        def mid(k1, c):
            base = pl.multiple_of(k1 * A_PITCH, 8)
            g = g_ref[k1]
            ak = a_scr[pl.ds(base, 2 * FFT_R), :].astype(BF16)
            xk = jnp.dot(g, ak, preferred_element_type=F32)
            kf = kf_ref[order, k1].astype(F32)
            xr, xi = xk[:FFT_R], xk[FFT_R:]
            kr, ki = kf[:FFT_R], kf[FFT_R:]
            y = jnp.concatenate([xr * kr - xi * ki, xr * ki + xi * kr], axis=0).astype(BF16)
            a_scr[pl.ds(base, 2 * FFT_R), :] = lax.dot_general(g, y, tn, preferred_element_type=F32)
            return c

        lax.fori_loop(0, FFT_K1, mid, 0)

        def inv2(n2, c):
            bn = jnp.concatenate([a_scr[pl.ds(n2, FFT_K1P, stride=A_PITCH), :],
                                  a_scr[pl.ds(FFT_R + n2, FFT_K1P, stride=A_PITCH), :]], axis=0)
            u_scr[pl.ds(n2, FFT_H, stride=U_PITCH), :] = jnp.dot(
                f1i, bn.astype(BF16), preferred_element_type=F32)
            return c

        lax.fori_loop(0, FFT_R, inv2, 0)

        cw, cb = cw_ref[...], cb_ref[...]

        def gate(n1, c):
            ub = pl.multiple_of(n1 * U_PITCH, 8)
            z = _conv3_chunk(gate_ref, n1, n_chunks, cw, cb) * u_scr[pl.ds(ub, FFT_R), :]
            if order == HY_ORDER - 1:
                o_ref[pl.ds(pl.multiple_of(n1 * FFT_R, FFT_R), FFT_R), :] = z.astype(o_ref.dtype)
            else:
                u_scr[pl.ds(ub, FFT_R), :] = z
            return c

        lax.fori_loop(0, n_chunks, gate, 0)


def _dft_tables():
    n = FFT_R * FFT_R
    k1 = np.arange(FFT_K1)[:, None]
    n1 = np.arange(FFT_H)[None, :]
    th = 2.0 * np.pi * k1 * n1 / FFT_R
    f1 = np.zeros((2 * FFT_K1P, FFT_H), np.float64)
    f1[:FFT_K1] = np.cos(th)
    f1[FFT_K1P:FFT_K1P + FFT_K1] = -np.sin(th)
    wgt = np.full((FFT_K1, 1), 2.0)
    wgt[0] = 1.0
    wgt[-1] = 1.0
    f1i = np.zeros((FFT_H, 2 * FFT_K1P), np.float64)
    f1i[:, :FFT_K1] = (wgt * np.cos(th)).T / n
    f1i[:, FFT_K1P:FFT_K1P + FFT_K1] = (-wgt * np.sin(th)).T / n
    kk = np.arange(FFT_K1)[:, None, None] + FFT_R * np.arange(FFT_R)[None, :, None]
    n2 = np.arange(FFT_R)[None, None, :]
    ph = 2.0 * np.pi * ((kk * n2) % n) / n
    gr, gi = np.cos(ph), -np.sin(ph)
    g = np.concatenate([np.concatenate([gr, -gi], axis=2), np.concatenate([gi, gr], axis=2)], axis=1)
    return (jnp.asarray(f1, F32).astype(BF16), jnp.asarray(g, F32).astype(BF16),
            jnp.asarray(f1i, F32).astype(BF16))


def _hyena_latent(p, conv_w, conv_b, kf):
    b, s, _ = p.shape
    assert s == FFT_H * FFT_R
    cb_ = LANES
    ncb = HY_W // cb_
    f1, g, f1i = _dft_tables()
    conv_b2 = conv_b.reshape(1, -1)
    seq = lambda off: pl.BlockSpec((None, s, cb_), lambda c, i: (i, 0, off + c))
    cws = lambda off: pl.BlockSpec((3, cb_), lambda c, i: (0, off + c))
    cbs = lambda off: pl.BlockSpec((1, cb_), lambda c, i: (0, off + c))
    return pl.pallas_call(
        _hyena_kernel,
        grid=(ncb, b),
        in_specs=[
            seq(0), seq(ncb), seq(2 * ncb),
            cws(0), cws(ncb), cws(2 * ncb),
            cbs(0), cbs(ncb), cbs(2 * ncb),
            _resident(f1.shape, lambda c, i: (0, 0)),
            _resident(g.shape, lambda c, i: (0, 0, 0)),
            _resident(f1i.shape, lambda c, i: (0, 0)),
            _resident((HY_ORDER, FFT_K1, 2 * FFT_R, cb_), lambda c, i: (0, 0, 0, c)),
        ],
        out_specs=pl.BlockSpec((None, s, cb_), lambda c, i: (i, 0, c)),
        out_shape=jax.ShapeDtypeStruct((b, s, HY_W), BF16),
        scratch_shapes=[
            pltpu.VMEM((FFT_H * U_PITCH, cb_), F32),
            pltpu.VMEM((FFT_K1P * A_PITCH, cb_), F32),
        ],
        compiler_params=_cparams(("arbitrary", "arbitrary")),
        name="hyena_latent",
    )(p, p, p, conv_w, conv_w, conv_w, conv_b2, conv_b2, conv_b2, f1, g, f1i, kf)


def _hyena_ctx_kernel(u_ref, cw_ref, cb_ref, fc_ref, fi_ref, kf_ref, o_ref):
    l = u_ref.shape[0]
    half = fc_ref.shape[0] // 2
    u = u_ref[...].astype(F32)
    row = lax.broadcasted_iota(jnp.int32, u.shape, 0)
    cw = cw_ref[...]
    um1 = jnp.where(row == 0, 0.0, pltpu.roll(u, 1, 0))
    up1 = jnp.where(row == l - 1, 0.0, pltpu.roll(u, l - 1, 0))
    uc = cw[0:1] * um1 + cw[1:2] * u + cw[2:3] * up1 + cb_ref[...]
    z = uc[:, :HY_W]
    hi = lax.Precision.HIGHEST
    for order in range(HY_ORDER):
        gate = uc[:, (order + 1) * HY_W:(order + 2) * HY_W]
        xf = jnp.dot(fc_ref[...], z, precision=hi, preferred_element_type=F32)
        kf = kf_ref[order]
        xr, xi = xf[:half], xf[half:]
        kr, ki = kf[:half], kf[half:]
        y = jnp.concatenate([xr * kr - xi * ki, xr * ki + xi * kr], axis=0)
        z = gate * jnp.dot(fi_ref[...], y, precision=hi, preferred_element_type=F32)
    o_ref[...] = z.astype(o_ref.dtype)


def _hyena_ctx(p, conv_w, conv_b, kf):
    b, l, _ = p.shape
    n = 2 * l
    half = l + 8
    k = np.arange(l + 1)[:, None]
    t = np.arange(l)[None, :]
    th = 2.0 * np.pi * ((k * t) % n) / n
    fc = np.zeros((2 * half, l), np.float64)
    fc[:l + 1] = np.cos(th)
    fc[half:half + l + 1] = -np.sin(th)
    wgt = np.full((l + 1, 1), 2.0)
    wgt[0] = 1.0
    wgt[-1] = 1.0
    fi = np.zeros((l, 2 * half), np.float64)
    fi[:, :l + 1] = (wgt * np.cos(th)).T / n
    fi[:, half:half + l + 1] = (-wgt * np.sin(th)).T / n
    wtot = 3 * HY_W
    return pl.pallas_call(
        _hyena_ctx_kernel,
        grid=(b,),
        in_specs=[
            pl.BlockSpec((None, l, wtot), lambda i: (i, 0, 0)),
            pl.BlockSpec((3, wtot), lambda i: (0, 0)),
            pl.BlockSpec((1, wtot), lambda i: (0, 0)),
            pl.BlockSpec((2 * half, l), lambda i: (0, 0)),
            pl.BlockSpec((l, 2 * half), lambda i: (0, 0)),
            pl.BlockSpec((HY_ORDER, 2 * half, HY_W), lambda i: (0, 0, 0)),
        ],
        out_specs=pl.BlockSpec((None, l, HY_W), lambda i: (i, 0, 0)),
        out_shape=jax.ShapeDtypeStruct((b, l, HY_W), BF16),
        compiler_params=_cparams(("parallel",)),
        name="hyena_ctx",
    )(p, conv_w, conv_b.reshape(1, -1), jnp.asarray(fc, F32), jnp.asarray(fi, F32), kf)


def _mix_ffn_kernel(*refs, recurrent):
    if recurrent:
        (x_ref, z_ref, hf_ref, hr_ref, gx_ref, wo_ref, g1_ref, gain_ref, sc_ref, sh_ref, g2_ref,
         w1_ref, w3_ref, w2_ref, o_ref, x1_scr, h_scr, acc_scr) = refs
    else:
        (x_ref, m_ref, wo_ref, g1_ref, gain_ref, sc_ref, sh_ref, g2_ref,
         w1_ref, w3_ref, w2_ref, o_ref, x1_scr, h_scr, acc_scr) = refs
    j = pl.program_id(2)

    @pl.when(j == 0)
    def _():
        if recurrent:
            hw = z_ref.shape[-1]
            rec = ((hf_ref[...] + hr_ref[...]) * jax.nn.gelu(gx_ref[...].astype(F32))).astype(BF16)
            mixed = (jnp.dot(z_ref[...], wo_ref[:hw, :], preferred_element_type=F32)
                     + jnp.dot(rec, wo_ref[hw:, :], preferred_element_type=F32))
        else:
            mixed = jnp.dot(m_ref[...], wo_ref[...], preferred_element_type=F32)
        x1 = x_ref[...] + g1_ref[...] * mixed
        x1_scr[...] = x1
        h_scr[...] = _norm_mod(x1, gain_ref[...], sc_ref[...], sh_ref[...]).astype(BF16)
        acc_scr[...] = jnp.zeros_like(acc_scr)

    h = h_scr[...]
    a = jnp.dot(h, w1_ref[...], preferred_element_type=F32)
    bgate = jnp.dot(h, w3_ref[...], preferred_element_type=F32)
    t = (jax.nn.silu(a) * bgate).astype(BF16)
    acc_scr[...] += jnp.dot(t, w2_ref[...], preferred_element_type=F32)

    @pl.when(j == pl.num_programs(2) - 1)
    def _():
        o_ref[...] = x1_scr[...] + g2_ref[...] * acc_scr[...]


def _mix_ffn(x, mixer_inputs, wo, g1, gain, sc, sh, g2, w1, w3, w2, tm, recurrent):
    b, s, d = x.shape
    ff = w1.shape[1]
    nff = 2
    tf = ff // nff
    row = lambda width, col=0: pl.BlockSpec((None, tm, width), lambda i, r, j: (i, r, col))
    per_b = pl.BlockSpec((None, 1, d), lambda i, r, j: (i, 0, 0))
    if recurrent:
        z, hf, hr, p = mixer_inputs
        mix_specs = [row(HY_W), row(LRU_W), row(LRU_W), row(LRU_W, (3 * HY_W + LRU_W) // LRU_W)]
        mix_args = [z, hf, hr, p]
    else:
        (m,) = mixer_inputs
        mix_specs = [row(d)]
        mix_args = [m]
    return pl.pallas_call(
        functools.partial(_mix_ffn_kernel, recurrent=recurrent),
        grid=(b, s // tm, nff),
        in_specs=[row(d)] + mix_specs + [
            _resident((d, d), lambda i, r, j: (0, 0)),
            per_b,
            pl.BlockSpec((1, d), lambda i, r, j: (0, 0)),
            per_b, per_b, per_b,
            pl.BlockSpec((d, tf), lambda i, r, j: (0, j)),
            pl.BlockSpec((d, tf), lambda i, r, j: (0, j)),
            pl.BlockSpec((tf, d), lambda i, r, j: (j, 0)),
        ],
        out_specs=row(d),
        out_shape=jax.ShapeDtypeStruct((b, s, d), F32),
        scratch_shapes=[
            pltpu.VMEM((tm, d), F32),
            pltpu.VMEM((tm, d), BF16),
            pltpu.VMEM((tm, d), F32),
        ],
        compiler_params=_cparams(("parallel", "parallel", "arbitrary")),
        name="mix_ffn_rec" if recurrent else "mix_ffn_attn",
    )(x, *mix_args, wo, g1, gain, sc, sh, g2, w1, w3, w2)


def _hyena_filter(l, w1, b1, w2, b2, w3, freq):
    hp = lax.Precision.HIGHEST
    t = jnp.linspace(0.0, 1.0, l, dtype=F32)[:, None]
    bands = jnp.linspace(1e-4, HY_BANDS - 1, HY_BANDS, dtype=F32)
    w = 2.0 * math.pi * jnp.arange(l, dtype=F32)[:, None] / l
    z = jnp.concatenate([t, jnp.cos(bands * w), -jnp.sin(bands * w)], axis=-1)
    h = jnp.sin(freq * (jnp.dot(z, w1, precision=hp) + b1))
    h = jnp.sin(freq * (jnp.dot(h, w2, precision=hp) + b2))
    h = jnp.dot(h, w3, precision=hp).reshape(l, 2, HY_ORDER, HY_W)
    max_decay = math.log(HY_DECAY_TARGET) / HY_FAST_DECAY
    min_decay = math.log(HY_DECAY_TARGET) / HY_SLOW_DECAY
    deltas = jnp.abs(jnp.linspace(min_decay, max_decay, HY_W, dtype=F32))
    window = jnp.exp(-t * deltas) + HY_MOD_SHIFT
    h = h * window[:, None, None, :]
    k = jnp.concatenate([h[:, 0], jnp.zeros((1, HY_ORDER, HY_W), F32), h[1:, 1][::-1]], axis=0)
    return k / jnp.sum(jnp.abs(k), axis=0, keepdims=True)


def _filter_spectrum(l, filt, bias):
    kf = jnp.fft.rfft(_hyena_filter(l, *filt), axis=0)
    return kf + bias[None].astype(kf.dtype)


def _spectrum_latent_layout(kf):
    full = jnp.concatenate([kf, jnp.conj(kf[-2:0:-1])], axis=0)
    full = full.reshape(FFT_R, FFT_R, HY_ORDER, HY_W)[:, :FFT_K1]
    full = jnp.transpose(full, (2, 1, 0, 3))
    return jnp.concatenate([full.real, full.imag], axis=2).astype(BF16)


def _spectrum_ctx_layout(kf):
    pad = ((0, 0), (0, 7), (0, 0))
    kt = jnp.transpose(kf, (1, 0, 2))
    return jnp.concatenate([jnp.pad(kt.real, pad), jnp.pad(kt.imag, pad)], axis=1).astype(F32)


def _block_diag(w):
    h, bw, _ = w.shape
    eye = jnp.eye(h, dtype=w.dtype)
    return (eye[:, None, :, None] * w[:, :, None, :]).reshape(h * bw, h * bw)


def _modulation(c_vec, w, b):
    m = jnp.dot(jax.nn.silu(c_vec), w, precision=lax.Precision.HIGHEST) + b
    return [t[:, None, :] for t in jnp.split(m, 6, axis=-1)]


def _rope_tables(s):
    half = HEAD_DIM // 2
    nf = half // 2
    inv = jnp.power(ROPE_BASE, -jnp.arange(nf, dtype=F32) / nf)
    pos = jnp.arange(s, dtype=jnp.int32)
    row = (pos // GRID_W).astype(F32)[:, None] * inv
    col = (pos % GRID_W).astype(F32)[:, None] * inv
    cos = jnp.concatenate([jnp.cos(row), jnp.cos(row), jnp.cos(col), jnp.cos(col)], axis=-1)
    sin = jnp.concatenate([-jnp.sin(row), jnp.sin(row), -jnp.sin(col), jnp.sin(col)], axis=-1)
    return jnp.tile(cos, (1, LANES // HEAD_DIM)), jnp.tile(sin, (1, LANES // HEAD_DIM))


def kernel(x, c, ctx, c_ctx, norm1, norm2, w_mod, b_mod, ffn_w1, ffn_w3, ffn_w2, ab_w_in, hy_conv_w, hy_conv_b, hy_f_w1, hy_f_b1, hy_f_w2, hy_f_b2, hy_f_w3, hy_f_freq, hy_bias, lru_conv_w, lru_conv_b, lru_w_a, lru_b_a, lru_w_i, lru_b_i, lru_lam, ab_w_out, at_w_qkv, at_q_gain, at_k_gain, at_sink, at_w_o):
    nb, s, d = x.shape
    cl = ctx.shape[1]
    bf = lambda a: a.astype(BF16)

    sh1x, sc1x, g1x, sh2x, sc2x, g2x = _modulation(c, w_mod[0], b_mod[0])
    mod_c = _modulation(jnp.broadcast_to(c_ctx[None], (nb, d)), w_mod[0], b_mod[0])
    sh1c, sc1c, g1c, sh2c, sc2c, g2c = mod_c
    gain1 = norm1[0][None]
    gain2 = norm2[0][None]
    w_in = bf(ab_w_in[0])
    px = _proj(x, gain1, sc1x, sh1x, w_in, 512)
    pc = _proj(ctx, gain1, sc1c, sh1c, w_in, cl)

    filt = (hy_f_w1[0], hy_f_b1[0], hy_f_w2[0], hy_f_b2[0], hy_f_w3[0], hy_f_freq[0])
    kf_x = _spectrum_latent_layout(_filter_spectrum(s, filt, hy_bias[0]))
    kf_c = _spectrum_ctx_layout(_filter_spectrum(cl, filt, hy_bias[0]))
    zx = _hyena_latent(px, hy_conv_w[0], hy_conv_b[0], kf_x)
    zc = _hyena_ctx(pc, hy_conv_w[0], hy_conv_b[0], kf_c)

    lru_col = 3 * HY_W // LRU_W
    lcb = lru_conv_b[0][None]
    h0 = jnp.zeros((nb, LRU_W), F32)
    hx, hc = [], []
    for dr, reverse in enumerate((False, True)):
        wg = bf(jnp.concatenate([_block_diag(lru_w_a[0, dr]), _block_diag(lru_w_i[0, dr])], axis=1))
        bg = jnp.concatenate([lru_b_a[0, dr], lru_b_i[0, dr]])[None]
        lam = lru_lam[0, dr][None]
        hcs, h_end = _lru(pc, lru_col, lru_conv_w[0], lcb, wg, bg, lam, h0, reverse, cl)
        hxs, _ = _lru(px, lru_col, lru_conv_w[0], lcb, wg, bg, lam, h_end, reverse, 256)
        hx.append(hxs)
        hc.append(hcs)

    w_out = bf(ab_w_out[0])
    w1, w3, w2 = bf(ffn_w1[0]), bf(ffn_w3[0]), bf(ffn_w2[0])
    x = _mix_ffn(x, (zx, hx[0], hx[1], px), w_out, g1x, gain2, sc2x, sh2x, g2x, w1, w3, w2, 512, True)
    ctx = _mix_ffn(ctx, (zc, hc[0], hc[1], pc), w_out, g1c, gain2, sc2c, sh2c, g2c, w1, w3, w2, cl, True)

    sh1x, sc1x, g1x, sh2x, sc2x, g2x = _modulation(c, w_mod[1], b_mod[1])
    sh1c, sc1c = _modulation(jnp.broadcast_to(c_ctx[None], (nb, d)), w_mod[1], b_mod[1])[:2]
    gain1 = norm1[1][None]
    gain2 = norm2[1][None]
    w_qkv = bf(at_w_qkv[0])
    qg = jnp.tile(at_q_gain[0], LANES // HEAD_DIM)[None]
    kg = jnp.tile(at_k_gain[0], LANES // HEAD_DIM)[None]
    cos, sin = _rope_tables(s)
    q, k, v = _qkv(x, gain1, sc1x, sh1x, w_qkv, qg, kg, cos, sin, 512)
    _, kx, vx = _qkv(ctx, gain1, sc1c, sh1c, w_qkv, qg, kg,
                     jnp.ones((cl, LANES), F32), jnp.zeros((cl, LANES), F32), cl)
    o = _attention(q, k, v, kx, vx, at_sink[0], WINDOW)
    w1, w3, w2 = bf(ffn_w1[1]), bf(ffn_w3[1]), bf(ffn_w2[1])
    return _mix_ffn(x, (o,), bf(at_w_o[0]), g1x, gain2, sc2x, sh2x, g2x, w1, w3, w2, 512, False)
```

```python
import functools
import math

import numpy as np
import jax
import jax.numpy as jnp
from jax import lax
from jax.experimental import pallas as pl
from jax.experimental.pallas import tpu as pltpu

F32 = jnp.float32
BF16 = jnp.bfloat16

EPS = 1e-6
D_MODEL = 1024
GRID_W = 64
HY_W = 512
HY_ORDER = 2
HY_BANDS = 16
HY_MOD_SHIFT = 0.05
HY_FAST_DECAY = 0.3
HY_SLOW_DECAY = 1.5
HY_DECAY_TARGET = 1e-2
LRU_W = 512
LRU_HEADS = 8
LRU_C = 8.0
HEAD_DIM = 64
N_HEADS = 16
N_KV = 4
GROUP = N_HEADS // N_KV
WINDOW = 128
ROPE_BASE = 10000.0
NEG_INF = -1e30

VMEM_LIMIT_BYTES = 58 * 1024 * 1024
LANES = 128

FFT_R = 128
FFT_H = FFT_R // 2
FFT_K1 = FFT_R // 2 + 1
FFT_K1P = 72
FFT_NG = 4
U_PITCH = FFT_R + 8
A_PITCH = 2 * FFT_R + 8


def _cparams(sem):
    return pltpu.CompilerParams(dimension_semantics=sem, vmem_limit_bytes=VMEM_LIMIT_BYTES)


def _resident(block_shape, index_map):
    return pl.BlockSpec(block_shape, index_map, pipeline_mode=pl.Buffered(1))


def _norm_mod(x, gain, scale, shift):
    y = x * lax.rsqrt(jnp.mean(x * x, axis=-1, keepdims=True) + EPS)
    return (y * gain) * (1.0 + scale) + shift


def _proj_kernel(x_ref, gain_ref, sc_ref, sh_ref, w_ref, o_ref, *, n_chunk):
    h = _norm_mod(x_ref[...], gain_ref[...], sc_ref[...], sh_ref[...]).astype(BF16)
    n_out = o_ref.shape[-1]
    for n0 in range(0, n_out, n_chunk):
        o_ref[:, n0:n0 + n_chunk] = jnp.dot(
            h, w_ref[:, n0:n0 + n_chunk], preferred_element_type=F32).astype(o_ref.dtype)


def _proj(x, gain, sc, sh, w, tm):
    b, s, d = x.shape
    n = w.shape[1]
    return pl.pallas_call(
        functools.partial(_proj_kernel, n_chunk=512),
        grid=(b, s // tm),
        in_specs=[
            pl.BlockSpec((None, tm, d), lambda i, j: (i, j, 0)),
            pl.BlockSpec((1, d), lambda i, j: (0, 0)),
            pl.BlockSpec((None, 1, d), lambda i, j: (i, 0, 0)),
            pl.BlockSpec((None, 1, d), lambda i, j: (i, 0, 0)),
            _resident((d, n), lambda i, j: (0, 0)),
        ],
        out_specs=pl.BlockSpec((None, tm, n), lambda i, j: (i, j, 0)),
        out_shape=jax.ShapeDtypeStruct((b, s, n), BF16),
        compiler_params=_cparams(("parallel", "parallel")),
        name="proj_in",
    )(x, gain, sc, sh, w)


def _qkv_kernel(x_ref, gain_ref, sc_ref, sh_ref, w_ref, qg_ref, kg_ref, cos_ref, sin_ref,
                q_ref, k_ref, v_ref):
    tm = x_ref.shape[0]
    nq = q_ref.shape[-1]
    nkv = k_ref.shape[-1]
    h = _norm_mod(x_ref[...], gain_ref[...], sc_ref[...], sh_ref[...]).astype(BF16)
    cos = cos_ref[...]
    sin = sin_ref[...]
    lane = lax.broadcasted_iota(jnp.int32, (tm, LANES), 1)
    low_head = lane < HEAD_DIM
    first_half = (lane % 32) < 16

    def norm_rope(y, g, scale):
        sq = y * y
        s_lo = jnp.sum(jnp.where(low_head, sq, 0.0), axis=-1, keepdims=True)
        s_hi = jnp.sum(jnp.where(low_head, 0.0, sq), axis=-1, keepdims=True)
        inv = lax.rsqrt(jnp.where(low_head, s_lo, s_hi) * (1.0 / HEAD_DIM) + EPS)
        y = (y * inv) * g
        partner = jnp.where(first_half, pltpu.roll(y, LANES - 16, 1), pltpu.roll(y, 16, 1))
        return (y * cos + partner * sin) * scale

    qg = qg_ref[...]
    kg = kg_ref[...]
    for n0 in range(0, nq, 512):
        y = jnp.dot(h, w_ref[:, n0:n0 + 512], preferred_element_type=F32)
        for j in range(4):
            q_ref[:, n0 + j * LANES:n0 + (j + 1) * LANES] = norm_rope(
                y[:, j * LANES:(j + 1) * LANES], qg, HEAD_DIM ** -0.5).astype(BF16)
    y = jnp.dot(h, w_ref[:, nq:nq + nkv], preferred_element_type=F32)
    for j in range(nkv // LANES):
        k_ref[:, j * LANES:(j + 1) * LANES] = norm_rope(
            y[:, j * LANES:(j + 1) * LANES], kg, 1.0).astype(BF16)
    v_ref[...] = jnp.dot(h, w_ref[:, nq + nkv:], preferred_element_type=F32).astype(BF16)


def _qkv(x, gain, sc, sh, w, qg, kg, cos, sin, tm):
    b, s, d = x.shape
    nq = N_HEADS * HEAD_DIM
    nkv = (w.shape[1] - nq) // 2
    return pl.pallas_call(
        _qkv_kernel,
        grid=(b, s // tm),
        in_specs=[
            pl.BlockSpec((None, tm, d), lambda i, j: (i, j, 0)),
            pl.BlockSpec((1, d), lambda i, j: (0, 0)),
            pl.BlockSpec((None, 1, d), lambda i, j: (i, 0, 0)),
            pl.BlockSpec((None, 1, d), lambda i, j: (i, 0, 0)),
            _resident((d, nq + 2 * nkv), lambda i, j: (0, 0)),
            pl.BlockSpec((1, LANES), lambda i, j: (0, 0)),
            pl.BlockSpec((1, LANES), lambda i, j: (0, 0)),
            pl.BlockSpec((tm, LANES), lambda i, j: (j, 0)),
            pl.BlockSpec((tm, LANES), lambda i, j: (j, 0)),
        ],
        out_specs=[
            pl.BlockSpec((None, tm, nq), lambda i, j: (i, j, 0)),
            pl.BlockSpec((None, tm, nkv), lambda i, j: (i, j, 0)),
            pl.BlockSpec((None, tm, nkv), lambda i, j: (i, j, 0)),
        ],
        out_shape=[
            jax.ShapeDtypeStruct((b, s, nq), BF16),
            jax.ShapeDtypeStruct((b, s, nkv), BF16),
            jax.ShapeDtypeStruct((b, s, nkv), BF16),
        ],
        compiler_params=_cparams(("parallel", "parallel")),
        name="qkv",
    )(x, gain, sc, sh, w, qg, kg, cos, sin)


def _attn_kernel(sink_ref, q_ref, kp_ref, kc_ref, kn_ref, vp_ref, vc_ref, vn_ref,
                 kx_ref, vx_ref, o_ref, *, seq):
    blk = q_ref.shape[0]
    n_loc = 3 * blk
    n_keys = n_loc + kx_ref.shape[0]
    start = pl.program_id(1) * blk
    rows = GROUP * blk
    r = lax.broadcasted_iota(jnp.int32, (rows, n_loc), 0) % blk
    c = lax.broadcasted_iota(jnp.int32, (rows, n_loc), 1)
    kpos = start - blk + c
    valid = (jnp.abs(c - blk - r) <= WINDOW) & (kpos >= 0) & (kpos < seq)
    bias = jnp.where(valid, 0.0, NEG_INF)
    lane = lax.broadcasted_iota(jnp.int32, (blk, LANES), 1)
    low = lane < HEAD_DIM
    keep_low = jnp.where(low, 1.0, 0.0).astype(BF16)
    keep_high = jnp.where(low, 0.0, 1.0).astype(BF16)
    ones = jnp.ones((n_keys, LANES), BF16)
    nt = (((1,), (1,)), ((), ()))
    for g in range(N_KV):
        gl = slice(g * LANES, (g + 1) * LANES)
        k_all = jnp.concatenate([kp_ref[:, gl], kc_ref[:, gl], kn_ref[:, gl], kx_ref[:, gl]], axis=0)
        v_all = jnp.concatenate([vp_ref[:, gl], vc_ref[:, gl], vn_ref[:, gl], vx_ref[:, gl]], axis=0)
        v_aug = jnp.concatenate([v_all, ones], axis=1)
        q4 = jnp.concatenate(
            [q_ref[:, ((GROUP * g + j) // 2) * LANES:((GROUP * g + j) // 2 + 1) * LANES]
             * (keep_low if j % 2 == 0 else keep_high) for j in range(GROUP)], axis=0)
        s = lax.dot_general(q4, k_all, nt, preferred_element_type=F32)
        s_loc = s[:, :n_loc] + bias
        s_ctx = s[:, n_loc:]
        s_sink = jnp.concatenate(
            [jnp.full((blk, 1), sink_ref[GROUP * g + j], F32) for j in range(GROUP)], axis=0)
        m = jnp.maximum(jnp.maximum(jnp.max(s_loc, axis=-1, keepdims=True),
                                    jnp.max(s_ctx, axis=-1, keepdims=True)), s_sink)
        p = jnp.concatenate([jnp.exp(s_loc - m), jnp.exp(s_ctx - m)], axis=1).astype(BF16)
        pv = jnp.dot(p, v_aug, preferred_element_type=F32)
        o4 = pv[:, :LANES] / (pv[:, LANES:] + jnp.exp(s_sink - m))
        for pair in range(GROUP // 2):
            slab = jnp.where(low, o4[2 * pair * blk:(2 * pair + 1) * blk],
                             o4[(2 * pair + 1) * blk:(2 * pair + 2) * blk])
            col = (GROUP // 2) * g + pair
            o_ref[:, col * LANES:(col + 1) * LANES] = slab.astype(BF16)


def _attention(q, k, v, kx, vx, sink, blk):
    b, s, nq = q.shape
    nkv = k.shape[-1]
    cx = kx.shape[1]
    nb = s // blk
    qspec = pl.BlockSpec((None, blk, nq), lambda i, j, *_: (i, j, 0))
    prev = pl.BlockSpec((None, blk, nkv), lambda i, j, *_: (i, jnp.maximum(j - 1, 0), 0))
    cur = pl.BlockSpec((None, blk, nkv), lambda i, j, *_: (i, j, 0))
    nxt = pl.BlockSpec((None, blk, nkv), lambda i, j, *_: (i, jnp.minimum(j + 1, nb - 1), 0))
    cxs = pl.BlockSpec((None, cx, nkv), lambda i, j, *_: (i, 0, 0))
    return pl.pallas_call(
        functools.partial(_attn_kernel, seq=s),
        grid_spec=pltpu.PrefetchScalarGridSpec(
            num_scalar_prefetch=1,
            grid=(b, nb),
            in_specs=[qspec, prev, cur, nxt, prev, cur, nxt, cxs, cxs],
            out_specs=pl.BlockSpec((None, blk, nq), lambda i, j, *_: (i, j, 0)),
        ),
        out_shape=jax.ShapeDtypeStruct((b, s, nq), BF16),
        compiler_params=_cparams(("parallel", "parallel")),
        name="window_attn",
    )(sink, q, k, k, k, v, v, v, kx, vx)


def _lru_kernel(prev_ref, cur_ref, next_ref, cw_ref, cb_ref, wg_ref, bg_ref, lam_ref, h0_ref,
                h_ref, hend_ref, a_scr, b_scr, carry, *, reverse, n_chunks):
    nb, t, w = cur_ref.shape
    pitch = t + 8
    n_lane_groups = w // LANES
    i = pl.program_id(0)
    chunk = (n_chunks - 1 - i) if reverse else i

    @pl.when(i == 0)
    def _():
        carry[...] = h0_ref[...]

    lam = lam_ref[...]
    softplus_neg_lam = jnp.maximum(-lam, 0.0) + jnp.log1p(jnp.exp(-jnp.abs(lam)))
    cw = cw_ref[...]
    cb = cb_ref[...]
    bg = bg_ref[...]
    row = lax.broadcasted_iota(jnp.int32, (t, w), 0)
    has_prev = chunk > 0
    has_next = chunk < n_chunks - 1
    for b in range(nb):
        cur = cur_ref[b].astype(F32)
        pv = prev_ref[b].astype(F32)
        nx = next_ref[b].astype(F32)
        pm2 = jnp.where(has_prev, pv[14:15], 0.0)
        pm1 = jnp.where(has_prev, pv[15:16], 0.0)
        nx0 = jnp.where(has_next, nx[0:1], 0.0)
        xm1 = jnp.where(row == 0, pm1, pltpu.roll(cur, 1, 0))
        xm2 = jnp.where(row == 0, pm2, jnp.where(row == 1, pm1, pltpu.roll(cur, 2, 0)))
        xp1 = jnp.where(row == t - 1, nx0, pltpu.roll(cur, t - 1, 0))
        x = cw[0:1] * xm2 + cw[1:2] * xm1 + cw[2:3] * cur + cw[3:4] * xp1 + cb
        g = jnp.dot(x.astype(BF16), wg_ref[...], preferred_element_type=F32) + bg
        r = jax.nn.sigmoid(g[:, :w])
        gate_i = jax.nn.sigmoid(g[:, w:])
        log_a = (-LRU_C) * r * softplus_neg_lam
        a = jnp.exp(log_a)
        bb = jnp.sqrt(1.0 - jnp.exp(2.0 * log_a)) * (gate_i * x)
        for gi in range(n_lane_groups):
            a_scr[gi, pl.ds(b * pitch, t), :] = a[:, gi * LANES:(gi + 1) * LANES]
            b_scr[gi, pl.ds(b * pitch, t), :] = bb[:, gi * LANES:(gi + 1) * LANES]

    def step(s, hs):
        tt = (t - 1 - s) if reverse else s
        out = []
        for gi in range(n_lane_groups):
            a_t = a_scr[gi, pl.ds(tt, nb, stride=pitch), :]
            b_t = b_scr[gi, pl.ds(tt, nb, stride=pitch), :]
            h_new = a_t * hs[gi] + b_t
            b_scr[gi, pl.ds(tt, nb, stride=pitch), :] = h_new
            out.append(h_new)
        return tuple(out)

    h_init = tuple(carry[:, gi * LANES:(gi + 1) * LANES] for gi in range(n_lane_groups))
    h_fin = lax.fori_loop(0, t, step, h_init, unroll=8)
    for gi in range(n_lane_groups):
        carry[:, gi * LANES:(gi + 1) * LANES] = h_fin[gi]
        hend_ref[:, gi * LANES:(gi + 1) * LANES] = h_fin[gi]
    for b in range(nb):
        for gi in range(n_lane_groups):
            h_ref[b, :, gi * LANES:(gi + 1) * LANES] = b_scr[gi, pl.ds(b * pitch, t), :]


def _lru(p, col_block, cw, cb, wg, bg, lam, h0, reverse, t):
    nb, s, _ = p.shape
    w = LRU_W
    n_chunks = s // t
    halo = 16
    tb = t // halo
    last_halo = s // halo - 1
    if reverse:
        cidx = lambda i: n_chunks - 1 - i
    else:
        cidx = lambda i: i
    kern = functools.partial(_lru_kernel, reverse=reverse, n_chunks=n_chunks)
    return pl.pallas_call(
        kern,
        grid=(n_chunks,),
        in_specs=[
            pl.BlockSpec((nb, halo, w), lambda i: (0, jnp.maximum(cidx(i) * tb - 1, 0), col_block)),
            pl.BlockSpec((nb, t, w), lambda i: (0, cidx(i), col_block)),
            pl.BlockSpec((nb, halo, w), lambda i: (0, jnp.minimum((cidx(i) + 1) * tb, last_halo), col_block)),
            pl.BlockSpec((4, w), lambda i: (0, 0)),
            pl.BlockSpec((1, w), lambda i: (0, 0)),
            _resident((w, 2 * w), lambda i: (0, 0)),
            pl.BlockSpec((1, 2 * w), lambda i: (0, 0)),
            pl.BlockSpec((1, w), lambda i: (0, 0)),
            pl.BlockSpec((nb, w), lambda i: (0, 0)),
        ],
        out_specs=[
            pl.BlockSpec((nb, t, w), lambda i: (0, cidx(i), 0)),
            pl.BlockSpec((nb, w), lambda i: (0, 0)),
        ],
        out_shape=[
            jax.ShapeDtypeStruct((nb, s, w), F32),
            jax.ShapeDtypeStruct((nb, w), F32),
        ],
        scratch_shapes=[
            pltpu.VMEM((w // LANES, nb * (t + 8), LANES), F32),
            pltpu.VMEM((w // LANES, nb * (t + 8), LANES), F32),
            pltpu.VMEM((nb, w), F32),
        ],
        compiler_params=_cparams(("arbitrary",)),
        name="rglru_rev" if reverse else "rglru_fwd",
    )(p, p, p, cw, cb, wg, bg, lam, h0)


def _conv3_chunk(ref, n1, n_chunks, w, b):
    rows = FFT_R
    base = pl.multiple_of(n1 * rows, rows)
    cur = ref[pl.ds(base, rows), :].astype(F32)
    pbase = pl.multiple_of(jnp.maximum(base - 16, 0), 16)
    nbase = pl.multiple_of(jnp.minimum(base + rows, (n_chunks - 1) * rows), 16)
    prev = jnp.where(n1 > 0, ref[pl.ds(pbase, 16), :].astype(F32)[15:16], 0.0)
    nxt = jnp.where(n1 < n_chunks - 1, ref[pl.ds(nbase, 16), :].astype(F32)[0:1], 0.0)
    row = lax.broadcasted_iota(jnp.int32, cur.shape, 0)
    xm1 = jnp.where(row == 0, prev, pltpu.roll(cur, 1, 0))
    xp1 = jnp.where(row == rows - 1, nxt, pltpu.roll(cur, rows - 1, 0))
    return w[0:1] * xm1 + w[1:2] * cur + w[2:3] * xp1 + b


def _hyena_kernel(v_ref, x1_ref, x2_ref, cwv_ref, cw1_ref, cw2_ref, cbv_ref, cb1_ref, cb2_ref,
                  f1_ref, g_ref, f1i_ref, kf_ref, o_ref, u_scr, a_scr):
    n_chunks = FFT_H
    cwv, cbv = cwv_ref[...], cbv_ref[...]

    def load_v(n1, c):
        u_scr[pl.ds(pl.multiple_of(n1 * U_PITCH, 8), FFT_R), :] = _conv3_chunk(v_ref, n1, n_chunks, cwv, cbv)
        return c

    lax.fori_loop(0, n_chunks, load_v, 0)
    f1 = f1_ref[...]
    f1i = f1i_ref[...]
    tn = (((0,), (0,)), ((), ()))

    for order, (gate_ref, cw_ref, cb_ref) in enumerate(((x1_ref, cw1_ref, cb1_ref), (x2_ref, cw2_ref, cb2_ref))):
        def fwd1(grp, c):
            n2 = grp * FFT_NG
            xs = jnp.concatenate(
                [u_scr[pl.ds(n2 + j, FFT_H, stride=U_PITCH), :] for j in range(FFT_NG)], axis=1).astype(BF16)
            a = jnp.dot(f1, xs, preferred_element_type=F32)
            for j in range(FFT_NG):
                lanes = slice(j * LANES, (j + 1) * LANES)
                a_scr[pl.ds(n2 + j, FFT_K1P, stride=A_PITCH), :] = a[:FFT_K1P, lanes]
                a_scr[pl.ds(FFT_R + n2 + j, FFT_K1P, stride=A_PITCH), :] = a[FFT_K1P:, lanes]
            return c

        lax.fori_loop(0, FFT_R // FFT_NG, fwd1, 0, unroll=2)

        def mid(k1, c):
            base = pl.multiple_of(k1 * A_PITCH, 8)
            g = g_ref[k1]
            ak = a_scr[pl.ds(base, 2 * FFT_R), :].astype(BF16)
            xk = jnp.dot(g, ak, preferred_element_type=F32)
            kf = kf_ref[order, k1].astype(F32)
            xr, xi = xk[:FFT_R], xk[FFT_R:]
            kr, ki = kf[:FFT_R], kf[FFT_R:]
            y = jnp.concatenate([xr * kr - xi * ki, xr * ki + xi * kr], axis=0).astype(BF16)
            a_scr[pl.ds(base, 2 * FFT_R), :] = lax.dot_general(g, y, tn, preferred_element_type=F32)
            return c

        lax.fori_loop(0, FFT_K1, mid, 0, unroll=5)

        def inv2(grp, c):
            n2 = grp * FFT_NG
            bn = jnp.concatenate(
                [jnp.concatenate([a_scr[pl.ds(n2 + j, FFT_K1P, stride=A_PITCH), :],
                                  a_scr[pl.ds(FFT_R + n2 + j, FFT_K1P, stride=A_PITCH), :]], axis=0)
                 for j in range(FFT_NG)], axis=1).astype(BF16)
            y = jnp.dot(f1i, bn, preferred_element_type=F32)
            for j in range(FFT_NG):
                u_scr[pl.ds(n2 + j, FFT_H, stride=U_PITCH), :] = y[:, j * LANES:(j + 1) * LANES]
            return c

        lax.fori_loop(0, FFT_R // FFT_NG, inv2, 0, unroll=2)

        cw, cb = cw_ref[...], cb_ref[...]

        def gate(n1, c):
            ub = pl.multiple_of(n1 * U_PITCH, 8)
            z = _conv3_chunk(gate_ref, n1, n_chunks, cw, cb) * u_scr[pl.ds(ub, FFT_R), :]
            if order == HY_ORDER - 1:
                o_ref[pl.ds(pl.multiple_of(n1 * FFT_R, FFT_R), FFT_R), :] = z.astype(o_ref.dtype)
            else:
                u_scr[pl.ds(ub, FFT_R), :] = z
            return c

        lax.fori_loop(0, n_chunks, gate, 0)


def _dft_tables():
    n = FFT_R * FFT_R
    k1 = np.arange(FFT_K1)[:, None]
    n1 = np.arange(FFT_H)[None, :]
    th = 2.0 * np.pi * k1 * n1 / FFT_R
    f1 = np.zeros((2 * FFT_K1P, FFT_H), np.float64)
    f1[:FFT_K1] = np.cos(th)
    f1[FFT_K1P:FFT_K1P + FFT_K1] = -np.sin(th)
    wgt = np.full((FFT_K1, 1), 2.0)
    wgt[0] = 1.0
    wgt[-1] = 1.0
    f1i = np.zeros((FFT_H, 2 * FFT_K1P), np.float64)
    f1i[:, :FFT_K1] = (wgt * np.cos(th)).T / n
    f1i[:, FFT_K1P:FFT_K1P + FFT_K1] = (-wgt * np.sin(th)).T / n
    kk = np.arange(FFT_K1)[:, None, None] + FFT_R * np.arange(FFT_R)[None, :, None]
    n2 = np.arange(FFT_R)[None, None, :]
    ph = 2.0 * np.pi * ((kk * n2) % n) / n
    gr, gi = np.cos(ph), -np.sin(ph)
    g = np.concatenate([np.concatenate([gr, -gi], axis=2), np.concatenate([gi, gr], axis=2)], axis=1)
    return (jnp.asarray(f1, F32).astype(BF16), jnp.asarray(g, F32).astype(BF16),
            jnp.asarray(f1i, F32).astype(BF16))


def _hyena_latent(p, conv_w, conv_b, kf):
    b, s, _ = p.shape
    assert s == FFT_H * FFT_R
    cb_ = LANES
    ncb = HY_W // cb_
    f1, g, f1i = _dft_tables()
    conv_b2 = conv_b.reshape(1, -1)
    seq = lambda off: pl.BlockSpec((None, s, cb_), lambda c, i: (i, 0, off + c))
    cws = lambda off: pl.BlockSpec((3, cb_), lambda c, i: (0, off + c))
    cbs = lambda off: pl.BlockSpec((1, cb_), lambda c, i: (0, off + c))
    return pl.pallas_call(
        _hyena_kernel,
        grid=(ncb, b),
        in_specs=[
            seq(0), seq(ncb), seq(2 * ncb),
            cws(0), cws(ncb), cws(2 * ncb),
            cbs(0), cbs(ncb), cbs(2 * ncb),
            _resident(f1.shape, lambda c, i: (0, 0)),
            _resident(g.shape, lambda c, i: (0, 0, 0)),
            _resident(f1i.shape, lambda c, i: (0, 0)),
            _resident((HY_ORDER, FFT_K1, 2 * FFT_R, cb_), lambda c, i: (0, 0, 0, c)),
        ],
        out_specs=pl.BlockSpec((None, s, cb_), lambda c, i: (i, 0, c)),
        out_shape=jax.ShapeDtypeStruct((b, s, HY_W), BF16),
        scratch_shapes=[
            pltpu.VMEM((FFT_H * U_PITCH, cb_), F32),
            pltpu.VMEM((FFT_K1P * A_PITCH, cb_), F32),
        ],
        compiler_params=_cparams(("arbitrary", "arbitrary")),
        name="hyena_latent",
    )(p, p, p, conv_w, conv_w, conv_w, conv_b2, conv_b2, conv_b2, f1, g, f1i, kf)


def _hyena_ctx_kernel(u_ref, cw_ref, cb_ref, fc_ref, fi_ref, kf_ref, o_ref):
    l = u_ref.shape[0]
    half = fc_ref.shape[0] // 2
    u = u_ref[...].astype(F32)
    row = lax.broadcasted_iota(jnp.int32, u.shape, 0)
    cw = cw_ref[...]
    um1 = jnp.where(row == 0, 0.0, pltpu.roll(u, 1, 0))
    up1 = jnp.where(row == l - 1, 0.0, pltpu.roll(u, l - 1, 0))
    uc = cw[0:1] * um1 + cw[1:2] * u + cw[2:3] * up1 + cb_ref[...]
    z = uc[:, :HY_W]
    hi = lax.Precision.HIGHEST
    for order in range(HY_ORDER):
        gate = uc[:, (order + 1) * HY_W:(order + 2) * HY_W]
        xf = jnp.dot(fc_ref[...], z, precision=hi, preferred_element_type=F32)
        kf = kf_ref[order]
        xr, xi = xf[:half], xf[half:]
        kr, ki = kf[:half], kf[half:]
        y = jnp.concatenate([xr * kr - xi * ki, xr * ki + xi * kr], axis=0)
        z = gate * jnp.dot(fi_ref[...], y, precision=hi, preferred_element_type=F32)
    o_ref[...] = z.astype(o_ref.dtype)


def _hyena_ctx(p, conv_w, conv_b, kf):
    b, l, _ = p.shape
    n = 2 * l
    half = l + 8
    k = np.arange(l + 1)[:, None]
    t = np.arange(l)[None, :]
    th = 2.0 * np.pi * ((k * t) % n) / n
    fc = np.zeros((2 * half, l), np.float64)
    fc[:l + 1] = np.cos(th)
    fc[half:half + l + 1] = -np.sin(th)
    wgt = np.full((l + 1, 1), 2.0)
    wgt[0] = 1.0
    wgt[-1] = 1.0
    fi = np.zeros((l, 2 * half), np.float64)
    fi[:, :l + 1] = (wgt * np.cos(th)).T / n
    fi[:, half:half + l + 1] = (-wgt * np.sin(th)).T / n
    wtot = 3 * HY_W
    return pl.pallas_call(
        _hyena_ctx_kernel,
        grid=(b,),
        in_specs=[
            pl.BlockSpec((None, l, wtot), lambda i: (i, 0, 0)),
            pl.BlockSpec((3, wtot), lambda i: (0, 0)),
            pl.BlockSpec((1, wtot), lambda i: (0, 0)),
            pl.BlockSpec((2 * half, l), lambda i: (0, 0)),
            pl.BlockSpec((l, 2 * half), lambda i: (0, 0)),
            pl.BlockSpec((HY_ORDER, 2 * half, HY_W), lambda i: (0, 0, 0)),
        ],
        out_specs=pl.BlockSpec((None, l, HY_W), lambda i: (i, 0, 0)),
        out_shape=jax.ShapeDtypeStruct((b, l, HY_W), BF16),
        compiler_params=_cparams(("parallel",)),
        name="hyena_ctx",
    )(p, conv_w, conv_b.reshape(1, -1), jnp.asarray(fc, F32), jnp.asarray(fi, F32), kf)


def _mix_ffn_kernel(*refs, recurrent):
    if recurrent:
        (x_ref, z_ref, hf_ref, hr_ref, gx_ref, wo_ref, g1_ref, gain_ref, sc_ref, sh_ref, g2_ref,
         w1_ref, w3_ref, w2_ref, o_ref, x1_scr, h_scr, acc_scr) = refs
    else:
        (x_ref, m_ref, wo_ref, g1_ref, gain_ref, sc_ref, sh_ref, g2_ref,
         w1_ref, w3_ref, w2_ref, o_ref, x1_scr, h_scr, acc_scr) = refs
    j = pl.program_id(2)

    @pl.when(j == 0)
    def _():
        if recurrent:
            hw = z_ref.shape[-1]
            rec = ((hf_ref[...] + hr_ref[...]) * jax.nn.gelu(gx_ref[...].astype(F32))).astype(BF16)
            mixed = (jnp.dot(z_ref[...], wo_ref[:hw, :], preferred_element_type=F32)
                     + jnp.dot(rec, wo_ref[hw:, :], preferred_element_type=F32))
        else:
            mixed = jnp.dot(m_ref[...], wo_ref[...], preferred_element_type=F32)
        x1 = x_ref[...] + g1_ref[...] * mixed
        x1_scr[...] = x1
        h_scr[...] = _norm_mod(x1, gain_ref[...], sc_ref[...], sh_ref[...]).astype(BF16)
        acc_scr[...] = jnp.zeros_like(acc_scr)

    h = h_scr[...]
    a = jnp.dot(h, w1_ref[...], preferred_element_type=F32)
    bgate = jnp.dot(h, w3_ref[...], preferred_element_type=F32)
    t = (jax.nn.silu(a) * bgate).astype(BF16)
    acc_scr[...] += jnp.dot(t, w2_ref[...], preferred_element_type=F32)

    @pl.when(j == pl.num_programs(2) - 1)
    def _():
        o_ref[...] = x1_scr[...] + g2_ref[...] * acc_scr[...]


def _mix_ffn(x, mixer_inputs, wo, g1, gain, sc, sh, g2, w1, w3, w2, tm, recurrent):
    b, s, d = x.shape
    ff = w1.shape[1]
    nff = 2
    tf = ff // nff
    row = lambda width, col=0: pl.BlockSpec((None, tm, width), lambda i, r, j: (i, r, col))
    per_b = pl.BlockSpec((None, 1, d), lambda i, r, j: (i, 0, 0))
    if recurrent:
        z, hf, hr, p = mixer_inputs
        mix_specs = [row(HY_W), row(LRU_W), row(LRU_W), row(LRU_W, (3 * HY_W + LRU_W) // LRU_W)]
        mix_args = [z, hf, hr, p]
    else:
        (m,) = mixer_inputs
        mix_specs = [row(d)]
        mix_args = [m]
    return pl.pallas_call(
        functools.partial(_mix_ffn_kernel, recurrent=recurrent),
        grid=(b, s // tm, nff),
        in_specs=[row(d)] + mix_specs + [
            _resident((d, d), lambda i, r, j: (0, 0)),
            per_b,
            pl.BlockSpec((1, d), lambda i, r, j: (0, 0)),
            per_b, per_b, per_b,
            pl.BlockSpec((d, tf), lambda i, r, j: (0, j)),
            pl.BlockSpec((d, tf), lambda i, r, j: (0, j)),
            pl.BlockSpec((tf, d), lambda i, r, j: (j, 0)),
        ],
        out_specs=row(d),
        out_shape=jax.ShapeDtypeStruct((b, s, d), F32),
        scratch_shapes=[
            pltpu.VMEM((tm, d), F32),
            pltpu.VMEM((tm, d), BF16),
            pltpu.VMEM((tm, d), F32),
        ],
        compiler_params=_cparams(("parallel", "parallel", "arbitrary")),
        name="mix_ffn_rec" if recurrent else "mix_ffn_attn",
    )(x, *mix_args, wo, g1, gain, sc, sh, g2, w1, w3, w2)


def _hyena_filter(l, w1, b1, w2, b2, w3, freq):
    hp = lax.Precision.HIGHEST
    t = jnp.linspace(0.0, 1.0, l, dtype=F32)[:, None]
    bands = jnp.linspace(1e-4, HY_BANDS - 1, HY_BANDS, dtype=F32)
    w = 2.0 * math.pi * jnp.arange(l, dtype=F32)[:, None] / l
    z = jnp.concatenate([t, jnp.cos(bands * w), -jnp.sin(bands * w)], axis=-1)
    h = jnp.sin(freq * (jnp.dot(z, w1, precision=hp) + b1))
    h = jnp.sin(freq * (jnp.dot(h, w2, precision=hp) + b2))
    h = jnp.dot(h, w3, precision=hp).reshape(l, 2, HY_ORDER, HY_W)
    max_decay = math.log(HY_DECAY_TARGET) / HY_FAST_DECAY
    min_decay = math.log(HY_DECAY_TARGET) / HY_SLOW_DECAY
    deltas = jnp.abs(jnp.linspace(min_decay, max_decay, HY_W, dtype=F32))
    window = jnp.exp(-t * deltas) + HY_MOD_SHIFT
    h = h * window[:, None, None, :]
    k = jnp.concatenate([h[:, 0], jnp.zeros((1, HY_ORDER, HY_W), F32), h[1:, 1][::-1]], axis=0)
    return k / jnp.sum(jnp.abs(k), axis=0, keepdims=True)


def _filter_spectrum(l, filt, bias):
    kf = jnp.fft.rfft(_hyena_filter(l, *filt), axis=0)
    return kf + bias[None].astype(kf.dtype)


def _spectrum_latent_layout(kf):
    full = jnp.concatenate([kf, jnp.conj(kf[-2:0:-1])], axis=0)
    full = full.reshape(FFT_R, FFT_R, HY_ORDER, HY_W)[:, :FFT_K1]
    full = jnp.transpose(full, (2, 1, 0, 3))
    return jnp.concatenate([full.real, full.imag], axis=2).astype(BF16)


def _spectrum_ctx_layout(kf):
    pad = ((0, 0), (0, 7), (0, 0))
    kt = jnp.transpose(kf, (1, 0, 2))
    return jnp.concatenate([jnp.pad(kt.real, pad), jnp.pad(kt.imag, pad)], axis=1).astype(F32)


def _block_diag(w):
    h, bw, _ = w.shape
    eye = jnp.eye(h, dtype=w.dtype)
    return (eye[:, None, :, None] * w[:, :, None, :]).reshape(h * bw, h * bw)


def _dup_kv_columns(w):
    nq = N_HEADS * HEAD_DIM
    d = w.shape[0]
    kv = w[:, nq:].reshape(d, 2 * N_KV, 1, HEAD_DIM)
    kv = jnp.broadcast_to(kv, (d, 2 * N_KV, LANES // HEAD_DIM, HEAD_DIM)).reshape(d, -1)
    return jnp.concatenate([w[:, :nq], kv], axis=1)


def _modulation(c_vec, w, b):
    m = jnp.dot(jax.nn.silu(c_vec), w, precision=lax.Precision.HIGHEST) + b
    return [t[:, None, :] for t in jnp.split(m, 6, axis=-1)]


def _rope_tables(s):
    half = HEAD_DIM // 2
    nf = half // 2
    inv = jnp.power(ROPE_BASE, -jnp.arange(nf, dtype=F32) / nf)
    pos = jnp.arange(s, dtype=jnp.int32)
    row = (pos // GRID_W).astype(F32)[:, None] * inv
    col = (pos % GRID_W).astype(F32)[:, None] * inv
    cos = jnp.concatenate([jnp.cos(row), jnp.cos(row), jnp.cos(col), jnp.cos(col)], axis=-1)
    sin = jnp.concatenate([-jnp.sin(row), jnp.sin(row), -jnp.sin(col), jnp.sin(col)], axis=-1)
    return jnp.tile(cos, (1, LANES // HEAD_DIM)), jnp.tile(sin, (1, LANES // HEAD_DIM))


def kernel(x, c, ctx, c_ctx, norm1, norm2, w_mod, b_mod, ffn_w1, ffn_w3, ffn_w2, ab_w_in, hy_conv_w, hy_conv_b, hy_f_w1, hy_f_b1, hy_f_w2, hy_f_b2, hy_f_w3, hy_f_freq, hy_bias, lru_conv_w, lru_conv_b, lru_w_a, lru_b_a, lru_w_i, lru_b_i, lru_lam, ab_w_out, at_w_qkv, at_q_gain, at_k_gain, at_sink, at_w_o):
    nb, s, d = x.shape
    cl = ctx.shape[1]
    bf = lambda a: a.astype(BF16)

    sh1x, sc1x, g1x, sh2x, sc2x, g2x = _modulation(c, w_mod[0], b_mod[0])
    mod_c = _modulation(jnp.broadcast_to(c_ctx[None], (nb, d)), w_mod[0], b_mod[0])
    sh1c, sc1c, g1c, sh2c, sc2c, g2c = mod_c
    gain1 = norm1[0][None]
    gain2 = norm2[0][None]
    w_in = bf(ab_w_in[0])
    px = _proj(x, gain1, sc1x, sh1x, w_in, 512)
    pc = _proj(ctx, gain1, sc1c, sh1c, w_in, cl)

    filt = (hy_f_w1[0], hy_f_b1[0], hy_f_w2[0], hy_f_b2[0], hy_f_w3[0], hy_f_freq[0])
    kf_x = _spectrum_latent_layout(_filter_spectrum(s, filt, hy_bias[0]))
    kf_c = _spectrum_ctx_layout(_filter_spectrum(cl, filt, hy_bias[0]))
    zx = _hyena_latent(px, hy_conv_w[0], hy_conv_b[0], kf_x)
    zc = _hyena_ctx(pc, hy_conv_w[0], hy_conv_b[0], kf_c)

    lru_col = 3 * HY_W // LRU_W
    lcb = lru_conv_b[0][None]
    h0 = jnp.zeros((nb, LRU_W), F32)
    hx, hc = [], []
    for dr, reverse in enumerate((False, True)):
        wg = bf(jnp.concatenate([_block_diag(lru_w_a[0, dr]), _block_diag(lru_w_i[0, dr])], axis=1))
        bg = jnp.concatenate([lru_b_a[0, dr], lru_b_i[0, dr]])[None]
        lam = lru_lam[0, dr][None]
        hcs, h_end = _lru(pc, lru_col, lru_conv_w[0], lcb, wg, bg, lam, h0, reverse, cl)
        hxs, _ = _lru(px, lru_col, lru_conv_w[0], lcb, wg, bg, lam, h_end, reverse, 256)
        hx.append(hxs)
        hc.append(hcs)

    w_out = bf(ab_w_out[0])
    w1, w3, w2 = bf(ffn_w1[0]), bf(ffn_w3[0]), bf(ffn_w2[0])
    x = _mix_ffn(x, (zx, hx[0], hx[1], px), w_out, g1x, gain2, sc2x, sh2x, g2x, w1, w3, w2, 512, True)
    ctx = _mix_ffn(ctx, (zc, hc[0], hc[1], pc), w_out, g1c, gain2, sc2c, sh2c, g2c, w1, w3, w2, cl, True)

    sh1x, sc1x, g1x, sh2x, sc2x, g2x = _modulation(c, w_mod[1], b_mod[1])
    sh1c, sc1c = _modulation(jnp.broadcast_to(c_ctx[None], (nb, d)), w_mod[1], b_mod[1])[:2]
    gain1 = norm1[1][None]
    gain2 = norm2[1][None]
    w_qkv = bf(_dup_kv_columns(at_w_qkv[0]))
    qg =jnp.tile(at_q_gain[0], LANES // HEAD_DIM)[None]
    kg = jnp.tile(at_k_gain[0], LANES // HEAD_DIM)[None]
    cos, sin = _rope_tables(s)
    q, k, v = _qkv(x, gain1, sc1x, sh1x, w_qkv, qg, kg, cos, sin, 512)
    _, kx, vx = _qkv(ctx, gain1, sc1c, sh1c, w_qkv, qg, kg,
                     jnp.ones((cl, LANES), F32), jnp.zeros((cl, LANES), F32), cl)
    o = _attention(q, k, v, kx, vx, at_sink[0], WINDOW)
    w1, w3, w2 = bf(ffn_w1[1]), bf(ffn_w3[1]), bf(ffn_w2[1])
    return _mix_ffn(x, (o,), bf(at_w_o[0]), g1x, gain2, sc2x, sh2x, g2x, w1, w3, w2, 512, False)
```

```python
import functools
import math

import numpy as np
import jax
import jax.numpy as jnp
from jax import lax
from jax.experimental import pallas as pl
from jax.experimental.pallas import tpu as pltpu

F32 = jnp.float32
BF16 = jnp.bfloat16

EPS = 1e-6
D_MODEL = 1024
GRID_W = 64
HY_W = 512
HY_ORDER = 2
HY_BANDS = 16
HY_MOD_SHIFT = 0.05
HY_FAST_DECAY = 0.3
HY_SLOW_DECAY = 1.5
HY_DECAY_TARGET = 1e-2
LRU_W = 512
LRU_HEADS = 8
LRU_C = 8.0
HEAD_DIM = 64
N_HEADS = 16
N_KV = 4
GROUP = N_HEADS // N_KV
WINDOW = 128
ROPE_BASE = 10000.0
NEG_INF = -1e30

VMEM_LIMIT_BYTES = 58 * 1024 * 1024
LANES = 128

FFT_R = 128
FFT_H = FFT_R // 2
FFT_K1 = FFT_R // 2 + 1
FFT_K1P = 72
FFT_NG = 4
FFN_HIDDEN_CHUNK = 512
FFN_OUT_CHUNK = 256
FILT_EMB_PAD = 40
U_PITCH = FFT_R + 8
A_PITCH = 2 * FFT_R + 8


def _cparams(sem):
    return pltpu.CompilerParams(dimension_semantics=sem, vmem_limit_bytes=VMEM_LIMIT_BYTES)


def _resident(block_shape, index_map):
    return pl.BlockSpec(block_shape, index_map, pipeline_mode=pl.Buffered(1))


def _norm_mod(x, gain, scale, shift):
    y = x * lax.rsqrt(jnp.mean(x * x, axis=-1, keepdims=True) + EPS)
    return (y * gain) * (1.0 + scale) + shift


def _proj_kernel(x_ref, gain_ref, sc_ref, sh_ref, w_ref, o_ref, *, n_chunk):
    h = _norm_mod(x_ref[...], gain_ref[...], sc_ref[...], sh_ref[...]).astype(BF16)
    n_out = o_ref.shape[-1]
    for n0 in range(0, n_out, n_chunk):
        o_ref[:, n0:n0 + n_chunk] = jnp.dot(
            h, w_ref[:, n0:n0 + n_chunk], preferred_element_type=F32).astype(o_ref.dtype)


def _proj(x, gain, sc, sh, w, tm):
    b, s, d = x.shape
    n = w.shape[1]
    return pl.pallas_call(
        functools.partial(_proj_kernel, n_chunk=512),
        grid=(b, s // tm),
        in_specs=[
            pl.BlockSpec((None, tm, d), lambda i, j: (i, j, 0)),
            pl.BlockSpec((1, d), lambda i, j: (0, 0)),
            pl.BlockSpec((None, 1, d), lambda i, j: (i, 0, 0)),
            pl.BlockSpec((None, 1, d), lambda i, j: (i, 0, 0)),
            _resident((d, n), lambda i, j: (0, 0)),
        ],
        out_specs=pl.BlockSpec((None, tm, n), lambda i, j: (i, j, 0)),
        out_shape=jax.ShapeDtypeStruct((b, s, n), BF16),
        compiler_params=_cparams(("parallel", "parallel")),
        name="proj_in",
    )(x, gain, sc, sh, w)


def _qkv_kernel(x_ref, gain_ref, sc_ref, sh_ref, w_ref, qg_ref, kg_ref, cos_ref, sin_ref,
                q_ref, k_ref, v_ref):
    tm = x_ref.shape[0]
    nq = q_ref.shape[-1]
    nkv = k_ref.shape[-1]
    h = _norm_mod(x_ref[...], gain_ref[...], sc_ref[...], sh_ref[...]).astype(BF16)
    cos = cos_ref[...]
    sin = sin_ref[...]
    lane = lax.broadcasted_iota(jnp.int32, (tm, LANES), 1)
    low_head = lane < HEAD_DIM
    first_half = (lane % 32) < 16

    def norm_rope(y, g, scale):
        sq = y * y
        s_lo = jnp.sum(jnp.where(low_head, sq, 0.0), axis=-1, keepdims=True)
        s_hi = jnp.sum(jnp.where(low_head, 0.0, sq), axis=-1, keepdims=True)
        inv = lax.rsqrt(jnp.where(low_head, s_lo, s_hi) * (1.0 / HEAD_DIM) + EPS)
        y = (y * inv) * g
        partner = jnp.where(first_half, pltpu.roll(y, LANES - 16, 1), pltpu.roll(y, 16, 1))
        return (y * cos + partner * sin) * scale

    qg = qg_ref[...]
    kg = kg_ref[...]
    for n0 in range(0, nq, 512):
        y = jnp.dot(h, w_ref[:, n0:n0 + 512], preferred_element_type=F32)
        for j in range(4):
            q_ref[:, n0 + j * LANES:n0 + (j + 1) * LANES] = norm_rope(
                y[:, j * LANES:(j + 1) * LANES], qg, HEAD_DIM ** -0.5).astype(BF16)
    y = jnp.dot(h, w_ref[:, nq:nq + nkv], preferred_element_type=F32)
    for j in range(nkv // LANES):
        k_ref[:, j * LANES:(j + 1) * LANES] = norm_rope(
            y[:, j * LANES:(j + 1) * LANES], kg, 1.0).astype(BF16)
    v_ref[...] = jnp.dot(h, w_ref[:, nq + nkv:], preferred_element_type=F32).astype(BF16)


def _qkv(x, gain, sc, sh, w, qg, kg, cos, sin, tm):
    b, s, d = x.shape
    nq = N_HEADS * HEAD_DIM
    nkv = (w.shape[1] - nq) // 2
    return pl.pallas_call(
        _qkv_kernel,
        grid=(b, s // tm),
        in_specs=[
            pl.BlockSpec((None, tm, d), lambda i, j: (i, j, 0)),
            pl.BlockSpec((1, d), lambda i, j: (0, 0)),
            pl.BlockSpec((None, 1, d), lambda i, j: (i, 0, 0)),
            pl.BlockSpec((None, 1, d), lambda i, j: (i, 0, 0)),
            _resident((d, nq + 2 * nkv), lambda i, j: (0, 0)),
            pl.BlockSpec((1, LANES), lambda i, j: (0, 0)),
            pl.BlockSpec((1, LANES), lambda i, j: (0, 0)),
            pl.BlockSpec((tm, LANES), lambda i, j: (j, 0)),
            pl.BlockSpec((tm, LANES), lambda i, j: (j, 0)),
        ],
        out_specs=[
            pl.BlockSpec((None, tm, nq), lambda i, j: (i, j, 0)),
            pl.BlockSpec((None, tm, nkv), lambda i, j: (i, j, 0)),
            pl.BlockSpec((None, tm, nkv), lambda i, j: (i, j, 0)),
        ],
        out_shape=[
            jax.ShapeDtypeStruct((b, s, nq), BF16),
            jax.ShapeDtypeStruct((b, s, nkv), BF16),
            jax.ShapeDtypeStruct((b, s, nkv), BF16),
        ],
        compiler_params=_cparams(("parallel", "parallel")),
        name="qkv",
    )(x, gain, sc, sh, w, qg, kg, cos, sin)


def _attn_kernel(sink_ref, q_ref, kp_ref, kc_ref, kn_ref, vp_ref, vc_ref, vn_ref,
                 kx_ref, vx_ref, o_ref, *, seq):
    blk = q_ref.shape[0]
    n_loc = 3 * blk
    n_keys = n_loc + kx_ref.shape[0]
    start = pl.program_id(1) * blk
    rows = GROUP * blk
    r = lax.broadcasted_iota(jnp.int32, (rows, n_loc), 0) % blk
    c = lax.broadcasted_iota(jnp.int32, (rows, n_loc), 1)
    kpos = start - blk + c
    valid = (jnp.abs(c - blk - r) <= WINDOW) & (kpos >= 0) & (kpos < seq)
    bias = jnp.where(valid, 0.0, NEG_INF)
    lane = lax.broadcasted_iota(jnp.int32, (blk, LANES), 1)
    low = lane < HEAD_DIM
    keep_low = jnp.where(low, 1.0, 0.0).astype(BF16)
    keep_high = jnp.where(low, 0.0, 1.0).astype(BF16)
    ones = jnp.ones((n_keys, LANES), BF16)
    nt = (((1,), (1,)), ((), ()))
    for g in range(N_KV):
        gl = slice(g * LANES, (g + 1) * LANES)
        k_all = jnp.concatenate([kp_ref[:, gl], kc_ref[:, gl], kn_ref[:, gl], kx_ref[:, gl]], axis=0)
        v_all = jnp.concatenate([vp_ref[:, gl], vc_ref[:, gl], vn_ref[:, gl], vx_ref[:, gl]], axis=0)
        v_aug = jnp.concatenate([v_all, ones], axis=1)
        q4 = jnp.concatenate(
            [q_ref[:, ((GROUP * g + j) // 2) * LANES:((GROUP * g + j) // 2 + 1) * LANES]
             * (keep_low if j % 2 == 0 else keep_high) for j in range(GROUP)], axis=0)
        s = lax.dot_general(q4, k_all, nt, preferred_element_type=F32)
        s_loc = s[:, :n_loc] + bias
        s_ctx = s[:, n_loc:]
        s_sink = jnp.concatenate(
            [jnp.full((blk, 1), sink_ref[GROUP * g + j], F32) for j in range(GROUP)], axis=0)
        m = jnp.maximum(jnp.maximum(jnp.max(s_loc, axis=-1, keepdims=True),
                                    jnp.max(s_ctx, axis=-1, keepdims=True)), s_sink)
        p = jnp.concatenate([jnp.exp(s_loc - m), jnp.exp(s_ctx - m)], axis=1).astype(BF16)
        pv = jnp.dot(p, v_aug, preferred_element_type=F32)
        o4 = pv[:, :LANES] / (pv[:, LANES:] + jnp.exp(s_sink - m))
        for pair in range(GROUP // 2):
            slab = jnp.where(low, o4[2 * pair * blk:(2 * pair + 1) * blk],
                             o4[(2 * pair + 1) * blk:(2 * pair + 2) * blk])
            col = (GROUP // 2) * g + pair
            o_ref[:, col * LANES:(col + 1) * LANES] = slab.astype(BF16)


def _attention(q, k, v, kx, vx, sink, blk):
    b, s, nq = q.shape
    nkv = k.shape[-1]
    cx = kx.shape[1]
    nb = s // blk
    qspec = pl.BlockSpec((None, blk, nq), lambda i, j, *_: (i, j, 0))
    prev = pl.BlockSpec((None, blk, nkv), lambda i, j, *_: (i, jnp.maximum(j - 1, 0), 0))
    cur = pl.BlockSpec((None, blk, nkv), lambda i, j, *_: (i, j, 0))
    nxt = pl.BlockSpec((None, blk, nkv), lambda i, j, *_: (i, jnp.minimum(j + 1, nb - 1), 0))
    cxs = pl.BlockSpec((None, cx, nkv), lambda i, j, *_: (i, 0, 0))
    return pl.pallas_call(
        functools.partial(_attn_kernel, seq=s),
        grid_spec=pltpu.PrefetchScalarGridSpec(
            num_scalar_prefetch=1,
            grid=(b, nb),
            in_specs=[qspec, prev, cur, nxt, prev, cur, nxt, cxs, cxs],
            out_specs=pl.BlockSpec((None, blk, nq), lambda i, j, *_: (i, j, 0)),
        ),
        out_shape=jax.ShapeDtypeStruct((b, s, nq), BF16),
        compiler_params=_cparams(("parallel", "parallel")),
        name="window_attn",
    )(sink, q, k, k, k, v, v, v, kx, vx)


def _lru_kernel(prev_ref, cur_ref, next_ref, cw_ref, cb_ref, wg_ref, bg_ref, lam_ref, h0_ref,
                h_ref, hend_ref, a_scr, b_scr, carry, *, reverse, n_chunks):
    nb, t, w = cur_ref.shape
    pitch = t + 8
    n_lane_groups = w // LANES
    i = pl.program_id(0)
    chunk = (n_chunks - 1 - i) if reverse else i

    @pl.when(i == 0)
    def _():
        carry[...] = h0_ref[...]

    lam = lam_ref[...]
    softplus_neg_lam = jnp.maximum(-lam, 0.0) + jnp.log1p(jnp.exp(-jnp.abs(lam)))
    cw = cw_ref[...]
    cb = cb_ref[...]
    bg = bg_ref[...]
    row = lax.broadcasted_iota(jnp.int32, (t, w), 0)
    has_prev = chunk > 0
    has_next = chunk < n_chunks - 1
    for b in range(nb):
        cur = cur_ref[b].astype(F32)
        pv = prev_ref[b].astype(F32)
        nx = next_ref[b].astype(F32)
        pm2 = jnp.where(has_prev, pv[14:15], 0.0)
        pm1 = jnp.where(has_prev, pv[15:16], 0.0)
        nx0 = jnp.where(has_next, nx[0:1], 0.0)
        xm1 = jnp.where(row == 0, pm1, pltpu.roll(cur, 1, 0))
        xm2 = jnp.where(row == 0, pm2, jnp.where(row == 1, pm1, pltpu.roll(cur, 2, 0)))
        xp1 = jnp.where(row == t - 1, nx0, pltpu.roll(cur, t - 1, 0))
        x = cw[0:1] * xm2 + cw[1:2] * xm1 + cw[2:3] * cur + cw[3:4] * xp1 + cb
        g = jnp.dot(x.astype(BF16), wg_ref[...], preferred_element_type=F32) + bg
        r = jax.nn.sigmoid(g[:, :w])
        gate_i = jax.nn.sigmoid(g[:, w:])
        log_a = (-LRU_C) * r * softplus_neg_lam
        a = jnp.exp(log_a)
        bb = jnp.sqrt(1.0 - jnp.exp(2.0 * log_a)) * (gate_i * x)
        for gi in range(n_lane_groups):
            a_scr[gi, pl.ds(b * pitch, t), :] = a[:, gi * LANES:(gi + 1) * LANES]
            b_scr[gi, pl.ds(b * pitch, t), :] = bb[:, gi * LANES:(gi + 1) * LANES]

    def step(s, hs):
        tt = (t - 1 - s) if reverse else s
        out = []
        for gi in range(n_lane_groups):
            a_t = a_scr[gi, pl.ds(tt, nb, stride=pitch), :]
            b_t = b_scr[gi, pl.ds(tt, nb, stride=pitch), :]
            h_new = a_t * hs[gi] + b_t
            b_scr[gi, pl.ds(tt, nb, stride=pitch), :] = h_new
            out.append(h_new)
        return tuple(out)

    h_init = tuple(carry[:, gi * LANES:(gi + 1) * LANES] for gi in range(n_lane_groups))
    h_fin = lax.fori_loop(0, t, step, h_init, unroll=8)
    for gi in range(n_lane_groups):
        carry[:, gi * LANES:(gi + 1) * LANES] = h_fin[gi]
        hend_ref[:, gi * LANES:(gi + 1) * LANES] = h_fin[gi]
    for b in range(nb):
        for gi in range(n_lane_groups):
            h_ref[b, :, gi * LANES:(gi + 1) * LANES] = b_scr[gi, pl.ds(b * pitch, t), :]


def _lru(p, col_block, cw, cb, wg, bg, lam, h0, reverse, t):
    nb, s, _ = p.shape
    w = LRU_W
    n_chunks = s // t
    halo = 16
    tb = t // halo
    last_halo = s // halo - 1
    if reverse:
        cidx = lambda i: n_chunks - 1 - i
    else:
        cidx = lambda i: i
    kern = functools.partial(_lru_kernel, reverse=reverse, n_chunks=n_chunks)
    return pl.pallas_call(
        kern,
        grid=(n_chunks,),
        in_specs=[
            pl.BlockSpec((nb, halo, w), lambda i: (0, jnp.maximum(cidx(i) * tb - 1, 0), col_block)),
            pl.BlockSpec((nb, t, w), lambda i: (0, cidx(i), col_block)),
            pl.BlockSpec((nb, halo, w), lambda i: (0, jnp.minimum((cidx(i) + 1) * tb, last_halo), col_block)),
            pl.BlockSpec((4, w), lambda i: (0, 0)),
            pl.BlockSpec((1, w), lambda i: (0, 0)),
            _resident((w, 2 * w), lambda i: (0, 0)),
            pl.BlockSpec((1, 2 * w), lambda i: (0, 0)),
            pl.BlockSpec((1, w), lambda i: (0, 0)),
            pl.BlockSpec((nb, w), lambda i: (0, 0)),
        ],
        out_specs=[
            pl.BlockSpec((nb, t, w), lambda i: (0, cidx(i), 0)),
            pl.BlockSpec((nb, w), lambda i: (0, 0)),
        ],
        out_shape=[
            jax.ShapeDtypeStruct((nb, s, w), F32),
            jax.ShapeDtypeStruct((nb, w), F32),
        ],
        scratch_shapes=[
            pltpu.VMEM((w // LANES, nb * (t + 8), LANES), F32),
            pltpu.VMEM((w // LANES, nb * (t + 8), LANES), F32),
            pltpu.VMEM((nb, w), F32),
        ],
        compiler_params=_cparams(("arbitrary",)),
        name="rglru_rev" if reverse else "rglru_fwd",
    )(p, p, p, cw, cb, wg, bg, lam, h0)


def _conv3_chunk(ref, n1, n_chunks, w, b):
    rows = FFT_R
    base = pl.multiple_of(n1 * rows, rows)
    cur = ref[pl.ds(base, rows), :].astype(F32)
    pbase = pl.multiple_of(jnp.maximum(base - 16, 0), 16)
    nbase = pl.multiple_of(jnp.minimum(base + rows, (n_chunks - 1) * rows), 16)
    prev = jnp.where(n1 > 0, ref[pl.ds(pbase, 16), :].astype(F32)[15:16], 0.0)
    nxt = jnp.where(n1 < n_chunks - 1, ref[pl.ds(nbase, 16), :].astype(F32)[0:1], 0.0)
    row = lax.broadcasted_iota(jnp.int32, cur.shape, 0)
    xm1 = jnp.where(row == 0, prev, pltpu.roll(cur, 1, 0))
    xp1 = jnp.where(row == rows - 1, nxt, pltpu.roll(cur, rows - 1, 0))
    return w[0:1] * xm1 + w[1:2] * cur + w[2:3] * xp1 + b


def _fwd_level1(u_scr, a_scr, f1):
    def body(grp, c):
        n2 = grp * FFT_NG
        xs = jnp.concatenate(
            [u_scr[pl.ds(n2 + j, FFT_H, stride=U_PITCH), :] for j in range(FFT_NG)], axis=1).astype(BF16)
        a = jnp.dot(f1, xs, preferred_element_type=F32)
        for j in range(FFT_NG):
            lanes = slice(j * LANES, (j + 1) * LANES)
            a_scr[pl.ds(n2 + j, FFT_K1P, stride=A_PITCH), :] = a[:FFT_K1P, lanes]
            a_scr[pl.ds(FFT_R + n2 + j, FFT_K1P, stride=A_PITCH), :] = a[FFT_K1P:, lanes]
        return c

    lax.fori_loop(0, FFT_R // FFT_NG, body, 0, unroll=2)


def _hyena_kernel(v_ref, x1_ref, x2_ref, cwv_ref, cw1_ref, cw2_ref, cbv_ref, cb1_ref, cb2_ref,
                  f1_ref, g_ref, f1i_ref, kf_ref, o_ref, u_scr, a_scr):
    n_chunks = FFT_H
    cwv, cbv = cwv_ref[...], cbv_ref[...]

    def load_v(n1, c):
        u_scr[pl.ds(pl.multiple_of(n1 * U_PITCH, 8), FFT_R), :] = _conv3_chunk(v_ref, n1, n_chunks, cwv, cbv)
        return c

    lax.fori_loop(0, n_chunks, load_v, 0)
    f1 = f1_ref[...]
    f1i = f1i_ref[...]
    tn = (((0,), (0,)), ((), ()))

    for order, (gate_ref, cw_ref, cb_ref) in enumerate(((x1_ref, cw1_ref, cb1_ref), (x2_ref, cw2_ref, cb2_ref))):
        _fwd_level1(u_scr, a_scr, f1)

        def mid(k1, c):
            base = pl.multiple_of(k1 * A_PITCH, 8)
            g = g_ref[k1]
            ak = a_scr[pl.ds(base, 2 * FFT_R), :].astype(BF16)
            xk = jnp.dot(g, ak, preferred_element_type=F32)
            kf = kf_ref[order, k1].astype(F32)
            xr, xi = xk[:FFT_R], xk[FFT_R:]
            kr, ki = kf[:FFT_R], kf[FFT_R:]
            y = jnp.concatenate([xr * kr - xi * ki, xr * ki + xi * kr], axis=0).astype(BF16)
            a_scr[pl.ds(base, 2 * FFT_R), :] = lax.dot_general(g, y, tn, preferred_element_type=F32)
            return c

        lax.fori_loop(0, FFT_K1, mid, 0, unroll=5)

        def inv2(grp, c):
            n2 = grp * FFT_NG
            bn = jnp.concatenate(
                [jnp.concatenate([a_scr[pl.ds(n2 + j, FFT_K1P, stride=A_PITCH), :],
                                  a_scr[pl.ds(FFT_R + n2 + j, FFT_K1P, stride=A_PITCH), :]], axis=0)
                 for j in range(FFT_NG)], axis=1).astype(BF16)
            y = jnp.dot(f1i, bn, preferred_element_type=F32)
            for j in range(FFT_NG):
                u_scr[pl.ds(n2 + j, FFT_H, stride=U_PITCH), :] = y[:, j * LANES:(j + 1) * LANES]
            return c

        lax.fori_loop(0, FFT_R // FFT_NG, inv2, 0, unroll=2)

        cw, cb = cw_ref[...], cb_ref[...]

        def gate(n1, c):
            ub = pl.multiple_of(n1 * U_PITCH, 8)
            z = _conv3_chunk(gate_ref, n1, n_chunks, cw, cb) * u_scr[pl.ds(ub, FFT_R), :]
            if order == HY_ORDER - 1:
                o_ref[pl.ds(pl.multiple_of(n1 * FFT_R, FFT_R), FFT_R), :] = z.astype(o_ref.dtype)
            else:
                u_scr[pl.ds(ub, FFT_R), :] = z
            return c

        lax.fori_loop(0, n_chunks, gate, 0)


def _dft_tables():
    n = FFT_R * FFT_R
    k1 = np.arange(FFT_K1)[:, None]
    n1 = np.arange(FFT_H)[None, :]
    th = 2.0 * np.pi * k1 * n1 / FFT_R
    f1 = np.zeros((2 * FFT_K1P, FFT_H), np.float64)
    f1[:FFT_K1] = np.cos(th)
    f1[FFT_K1P:FFT_K1P + FFT_K1] = -np.sin(th)
    wgt = np.full((FFT_K1, 1), 2.0)
    wgt[0] = 1.0
    wgt[-1] = 1.0
    f1i = np.zeros((FFT_H, 2 * FFT_K1P), np.float64)
    f1i[:, :FFT_K1] = (wgt * np.cos(th)).T / n
    f1i[:, FFT_K1P:FFT_K1P + FFT_K1] = (-wgt * np.sin(th)).T / n
    kk = np.arange(FFT_K1)[:, None, None] + FFT_R * np.arange(FFT_R)[None, :, None]
    n2 = np.arange(FFT_R)[None, None, :]
    ph = 2.0 * np.pi * ((kk * n2) % n) / n
    gr, gi = np.cos(ph), -np.sin(ph)
    g = np.concatenate([np.concatenate([gr, -gi], axis=2), np.concatenate([gi, gr], axis=2)], axis=1)
    return (jnp.asarray(f1, F32).astype(BF16), jnp.asarray(g, F32).astype(BF16),
            jnp.asarray(f1i, F32).astype(BF16))


def _hyena_latent(p, conv_w, conv_b, kf):
    b, s, _ = p.shape
    assert s == FFT_H * FFT_R
    cb_ = LANES
    ncb = HY_W // cb_
    f1, g, f1i = _dft_tables()
    conv_b2 = conv_b.reshape(1, -1)
    seq = lambda off: pl.BlockSpec((None, s, cb_), lambda c, i: (i, 0, off + c))
    cws = lambda off: pl.BlockSpec((3, cb_), lambda c, i: (0, off + c))
    cbs = lambda off: pl.BlockSpec((1, cb_), lambda c, i: (0, off + c))
    return pl.pallas_call(
        _hyena_kernel,
        grid=(ncb, b),
        in_specs=[
            seq(0), seq(ncb), seq(2 * ncb),
            cws(0), cws(ncb), cws(2 * ncb),
            cbs(0), cbs(ncb), cbs(2 * ncb),
            _resident(f1.shape, lambda c, i: (0, 0)),
            _resident(g.shape, lambda c, i: (0, 0, 0)),
            _resident(f1i.shape, lambda c, i: (0, 0)),
            _resident((HY_ORDER, FFT_K1, 2 * FFT_R, cb_), lambda c, i: (0, 0, 0, c)),
        ],
        out_specs=pl.BlockSpec((None, s, cb_), lambda c, i: (i, 0, c)),
        out_shape=jax.ShapeDtypeStruct((b, s, HY_W), BF16),
        scratch_shapes=[
            pltpu.VMEM((FFT_H * U_PITCH, cb_), F32),
            pltpu.VMEM((FFT_K1P * A_PITCH, cb_), F32),
        ],
        compiler_params=_cparams(("arbitrary", "arbitrary")),
        name="hyena_latent",
    )(p, p, p, conv_w, conv_w, conv_w, conv_b2, conv_b2, conv_b2, f1, g, f1i, kf)


def _hyena_ctx_kernel(u_ref, cw_ref, cb_ref, fc_ref, fi_ref, kf_ref, o_ref):
    l = u_ref.shape[0]
    half = fc_ref.shape[0] // 2
    u = u_ref[...].astype(F32)
    row = lax.broadcasted_iota(jnp.int32, u.shape, 0)
    cw = cw_ref[...]
    um1 = jnp.where(row == 0, 0.0, pltpu.roll(u, 1, 0))
    up1 = jnp.where(row == l - 1, 0.0, pltpu.roll(u, l - 1, 0))
    uc = cw[0:1] * um1 + cw[1:2] * u + cw[2:3] * up1 + cb_ref[...]
    z = uc[:, :HY_W]
    hi = lax.Precision.HIGHEST
    for order in range(HY_ORDER):
        gate = uc[:, (order + 1) * HY_W:(order + 2) * HY_W]
        xf = jnp.dot(fc_ref[...], z, precision=hi, preferred_element_type=F32)
        kf = kf_ref[order]
        xr, xi = xf[:half], xf[half:]
        kr, ki = kf[:half], kf[half:]
        y = jnp.concatenate([xr * kr - xi * ki, xr * ki + xi * kr], axis=0)
        z = gate * jnp.dot(fi_ref[...], y, precision=hi, preferred_element_type=F32)
    o_ref[...] = z.astype(o_ref.dtype)


def _ctx_dft_tables(l):
    n = 2 * l
    half = l + 8
    k = np.arange(l + 1)[:, None]
    t = np.arange(l)[None, :]
    th = 2.0 * np.pi * ((k * t) % n) / n
    fc = np.zeros((2 * half, l), np.float64)
    fc[:l + 1] = np.cos(th)
    fc[half:half + l + 1] = -np.sin(th)
    wgt = np.full((l + 1, 1), 2.0)
    wgt[0] = 1.0
    wgt[-1] = 1.0
    fi = np.zeros((l, 2 * half), np.float64)
    fi[:, :l + 1] = (wgt * np.cos(th)).T / n
    fi[:, half:half + l + 1] = (-wgt * np.sin(th)).T / n
    return jnp.asarray(fc, F32), jnp.asarray(fi, F32)


def _hyena_ctx(p, conv_w, conv_b, kf, fc, fi):
    b, l, _ = p.shape
    half = fc.shape[0] // 2
    wtot = 3 * HY_W
    return pl.pallas_call(
        _hyena_ctx_kernel,
        grid=(b,),
        in_specs=[
            pl.BlockSpec((None, l, wtot), lambda i: (i, 0, 0)),
            pl.BlockSpec((3, wtot), lambda i: (0, 0)),
            pl.BlockSpec((1, wtot), lambda i: (0, 0)),
            pl.BlockSpec((2 * half, l), lambda i: (0, 0)),
            pl.BlockSpec((l, 2 * half), lambda i: (0, 0)),
            pl.BlockSpec((HY_ORDER, 2 * half, HY_W), lambda i: (0, 0, 0)),
        ],
        out_specs=pl.BlockSpec((None, l, HY_W), lambda i: (i, 0, 0)),
        out_shape=jax.ShapeDtypeStruct((b, l, HY_W), BF16),
        compiler_params=_cparams(("parallel",)),
        name="hyena_ctx",
    )(p, conv_w, conv_b.reshape(1, -1), fc, fi, kf)


def _mix_ffn_kernel(*refs, recurrent):
    if recurrent:
        (x_ref, z_ref, hf_ref, hr_ref, gx_ref, wo_ref, g1_ref, gain_ref, sc_ref, sh_ref, g2_ref,
         w1_ref, w3_ref, w2_ref, o_ref, x1_scr, h_scr, t_scr) = refs
    else:
        (x_ref, m_ref, wo_ref, g1_ref, gain_ref, sc_ref, sh_ref, g2_ref,
         w1_ref, w3_ref, w2_ref, o_ref, x1_scr, h_scr, t_scr) = refs
    if recurrent:
        hw = z_ref.shape[-1]
        rec = ((hf_ref[...] + hr_ref[...]) * jax.nn.gelu(gx_ref[...].astype(F32))).astype(BF16)
        mixed = (jnp.dot(z_ref[...], wo_ref[:hw, :], preferred_element_type=F32)
                 + jnp.dot(rec, wo_ref[hw:, :], preferred_element_type=F32))
    else:
        mixed = jnp.dot(m_ref[...], wo_ref[...], preferred_element_type=F32)
    x1 = x_ref[...] + g1_ref[...] * mixed
    x1_scr[...] = x1
    h_scr[...] = _norm_mod(x1, gain_ref[...], sc_ref[...], sh_ref[...]).astype(BF16)

    d, ff = w1_ref.shape
    for c0 in range(0, ff, FFN_HIDDEN_CHUNK):
        cols = slice(c0, min(c0 + FFN_HIDDEN_CHUNK, ff))
        h = h_scr[...]
        a = jnp.dot(h, w1_ref[:, cols], preferred_element_type=F32)
        bgate = jnp.dot(h, w3_ref[:, cols], preferred_element_type=F32)
        t_scr[:, cols] = (jax.nn.silu(a) * bgate).astype(BF16)
    for n0 in range(0, d, FFN_OUT_CHUNK):
        cols = slice(n0, n0 + FFN_OUT_CHUNK)
        y = jnp.dot(t_scr[...], w2_ref[:, cols], preferred_element_type=F32)
        o_ref[:, cols] = x1_scr[:, cols] + g2_ref[:, cols] * y


def _mix_ffn(x, mixer_inputs, wo, g1, gain, sc, sh, g2, w1, w3, w2, tm, recurrent):
    b, s, d = x.shape
    ff = w1.shape[1]
    row = lambda width, col=0: pl.BlockSpec((None, tm, width), lambda i, r: (i, r, col))
    per_b = pl.BlockSpec((None, 1, d), lambda i, r: (i, 0, 0))
    if recurrent:
        z, hf, hr, p = mixer_inputs
        mix_specs = [row(HY_W), row(LRU_W), row(LRU_W), row(LRU_W, (3 * HY_W + LRU_W) // LRU_W)]
        mix_args = [z, hf, hr, p]
    else:
        (m,) = mixer_inputs
        mix_specs = [row(d)]
        mix_args = [m]
    return pl.pallas_call(
        functools.partial(_mix_ffn_kernel, recurrent=recurrent),
        grid=(b, s // tm),
        in_specs=[row(d)] + mix_specs + [
            _resident((d, d), lambda i, r: (0, 0)),
            per_b,
            pl.BlockSpec((1, d), lambda i, r: (0, 0)),
            per_b, per_b, per_b,
            _resident((d, ff), lambda i, r: (0, 0)),
            _resident((d, ff), lambda i, r: (0, 0)),
            _resident((ff, d), lambda i, r: (0, 0)),
        ],
        out_specs=row(d),
        out_shape=jax.ShapeDtypeStruct((b, s, d), F32),
        scratch_shapes=[
            pltpu.VMEM((tm, d), F32),
            pltpu.VMEM((tm, d), BF16),
            pltpu.VMEM((tm, ff), BF16),
        ],
        compiler_params=_cparams(("parallel", "parallel")),
        name="mix_ffn_rec" if recurrent else "mix_ffn_attn",
    )(x, *mix_args, wo, g1, gain, sc, sh, g2, w1, w3, w2)


def _filter_features(l):
    t = jnp.linspace(0.0, 1.0, l, dtype=F32)[:, None]
    bands = jnp.linspace(1e-4, HY_BANDS - 1, HY_BANDS, dtype=F32)
    w = 2.0 * math.pi * jnp.arange(l, dtype=F32)[:, None] / l
    z = jnp.concatenate([t, jnp.cos(bands * w), -jnp.sin(bands * w)], axis=-1)
    return jnp.pad(z, ((0, 0), (0, FILT_EMB_PAD - z.shape[1])))


def _filter_mlp_kernel(z_ref, w1_ref, b1_ref, w2_ref, b2_ref, fr_ref, o_ref):
    hp = lax.Precision.HIGHEST
    fr = fr_ref[...]
    h = jnp.sin(fr * (jnp.dot(z_ref[...], w1_ref[...], precision=hp, preferred_element_type=F32) + b1_ref[...]))
    o_ref[...] = jnp.sin(fr * (jnp.dot(h, w2_ref[...], precision=hp, preferred_element_type=F32) + b2_ref[...]))


def _filter_mlp(l, w1, b1, w2, b2, freq):
    z = _filter_features(l)
    w1p = jnp.pad(w1, ((0, FILT_EMB_PAD - w1.shape[0]), (0, 0)))
    hid = w2.shape[0]
    return pl.pallas_call(
        _filter_mlp_kernel,
        out_shape=jax.ShapeDtypeStruct((l, hid), F32),
        compiler_params=pltpu.CompilerParams(vmem_limit_bytes=VMEM_LIMIT_BYTES),
        name="filter_mlp",
    )(z, w1p, b1[None], w2, b2[None], freq[None])


def _filter_decay():
    max_decay = math.log(HY_DECAY_TARGET) / HY_FAST_DECAY
    min_decay = math.log(HY_DECAY_TARGET) / HY_SLOW_DECAY
    return jnp.abs(jnp.linspace(min_decay, max_decay, HY_W, dtype=F32))[None]


def _filter_taps(h, t, w3f, w3b, decay, row0):
    hp = lax.Precision.HIGHEST
    window = jnp.exp(-t * decay) + HY_MOD_SHIFT
    hf = jnp.dot(h, w3f, precision=hp, preferred_element_type=F32) * window
    hb = jnp.dot(h, w3b, precision=hp, preferred_element_type=F32) * window
    row = lax.broadcasted_iota(jnp.int32, hb.shape, 0) + row0
    return hf, jnp.where(row == 0, 0.0, hb)


def _filter_spec_latent_kernel(h_ref, t_ref, w3f_ref, w3b_ref, dec_ref, bias_ref, f1_ref, g_ref,
                               o_ref, uf_scr, ub_scr, af_scr, ab_scr):
    w3f, w3b, dec = w3f_ref[...], w3b_ref[...], dec_ref[...]

    def load(n1, acc):
        base = pl.multiple_of(n1 * FFT_R, FFT_R)
        hf, hb = _filter_taps(h_ref[pl.ds(base, FFT_R), :], t_ref[pl.ds(base, FFT_R), :], w3f, w3b, dec, base)
        ub = pl.multiple_of(n1 * U_PITCH, 8)
        uf_scr[pl.ds(ub, FFT_R), :] = hf
        ub_scr[pl.ds(ub, FFT_R), :] = hb
        return acc + jnp.sum(jnp.abs(hf) + jnp.abs(hb), axis=0, keepdims=True)

    l1 = lax.fori_loop(0, FFT_H, load, jnp.zeros((1, LANES), F32))
    inv = 1.0 / l1
    bias = bias_ref[...]
    f1 = f1_ref[...]
    _fwd_level1(uf_scr, af_scr, f1)
    _fwd_level1(ub_scr, ab_scr, f1)

    def level2(k1, c):
        base = pl.multiple_of(k1 * A_PITCH, 8)
        g = g_ref[k1]
        xf = jnp.dot(g, af_scr[pl.ds(base, 2 * FFT_R), :].astype(BF16), preferred_element_type=F32)
        xb = jnp.dot(g, ab_scr[pl.ds(base, 2 * FFT_R), :].astype(BF16), preferred_element_type=F32)
        kr = (xf[:FFT_R] + xb[:FFT_R]) * inv + bias
        ki = (xf[FFT_R:] - xb[FFT_R:]) * inv
        o_ref[k1] = jnp.concatenate([kr, ki], axis=0).astype(o_ref.dtype)
        return c

    lax.fori_loop(0, FFT_K1, level2, 0, unroll=5)


def _filter_spec_latent(h, w3, bias):
    l, hid = h.shape
    assert l == FFT_H * FFT_R
    f1, g, _ = _dft_tables()
    ncb = HY_W // LANES
    t = jnp.linspace(0.0, 1.0, l, dtype=F32)[:, None]
    return pl.pallas_call(
        _filter_spec_latent_kernel,
        grid=(HY_ORDER * ncb,),
        in_specs=[
            pl.BlockSpec((l, hid), lambda i: (0, 0)),
            pl.BlockSpec((l, 1), lambda i: (0, 0)),
            pl.BlockSpec((hid, LANES), lambda i: (0, i)),
            pl.BlockSpec((hid, LANES), lambda i: (0, HY_ORDER * ncb + i)),
            pl.BlockSpec((1, LANES), lambda i: (0, i % ncb)),
            pl.BlockSpec((None, 1, LANES), lambda i: (i // ncb, 0, i % ncb)),
            _resident(f1.shape, lambda i: (0, 0)),
            _resident(g.shape, lambda i: (0, 0, 0)),
        ],
        out_specs=pl.BlockSpec((None, FFT_K1, 2 * FFT_R, LANES), lambda i: (i // ncb, 0, 0, i % ncb)),
        out_shape=jax.ShapeDtypeStruct((HY_ORDER, FFT_K1, 2 * FFT_R, HY_W), BF16),
        scratch_shapes=[
            pltpu.VMEM((FFT_H * U_PITCH, LANES), F32),
            pltpu.VMEM((FFT_H * U_PITCH, LANES), F32),
            pltpu.VMEM((FFT_K1P * A_PITCH, LANES), F32),
            pltpu.VMEM((FFT_K1P * A_PITCH, LANES), F32),
        ],
        compiler_params=_cparams(("arbitrary",)),
        name="filter_spec_latent",
    )(h, t, w3, w3, _filter_decay(), bias[:, None, :], f1, g)


def _filter_spec_ctx_kernel(h_ref, t_ref, w3f_ref, w3b_ref, dec_ref, bias_ref, fc_ref, o_ref):
    hp = lax.Precision.HIGHEST
    half = fc_ref.shape[0] // 2
    hf, hb = _filter_taps(h_ref[...], t_ref[...], w3f_ref[...], w3b_ref[...], dec_ref[...], 0)
    inv = 1.0 / jnp.sum(jnp.abs(hf) + jnp.abs(hb), axis=0, keepdims=True)
    xf = jnp.dot(fc_ref[...], hf, precision=hp, preferred_element_type=F32)
    xb = jnp.dot(fc_ref[...], hb, precision=hp, preferred_element_type=F32)
    kr = (xf[:half] + xb[:half]) * inv + bias_ref[...]
    ki = (xf[half:] - xb[half:]) * inv
    o_ref[...] = jnp.concatenate([kr, ki], axis=0)


def _filter_spec_ctx(h, w3, bias, fc):
    l, hid = h.shape
    ncb = HY_W // LANES
    t = jnp.linspace(0.0, 1.0, l, dtype=F32)[:, None]
    return pl.pallas_call(
        _filter_spec_ctx_kernel,
        grid=(HY_ORDER * ncb,),
        in_specs=[
            pl.BlockSpec((l, hid), lambda i: (0, 0)),
            pl.BlockSpec((l, 1), lambda i: (0, 0)),
            pl.BlockSpec((hid, LANES), lambda i: (0, i)),
            pl.BlockSpec((hid, LANES), lambda i: (0, HY_ORDER * ncb + i)),
            pl.BlockSpec((1, LANES), lambda i: (0, i % ncb)),
            pl.BlockSpec((None, 1, LANES), lambda i: (i // ncb, 0, i % ncb)),
            pl.BlockSpec(fc.shape, lambda i: (0, 0)),
        ],
        out_specs=pl.BlockSpec((None, fc.shape[0], LANES), lambda i: (i // ncb, 0, i % ncb)),
        out_shape=jax.ShapeDtypeStruct((HY_ORDER, fc.shape[0], HY_W), F32),
        compiler_params=_cparams(("parallel",)),
        name="filter_spec_ctx",
    )(h, t, w3, w3, _filter_decay(), bias[:, None, :], fc)


def _mod_kernel(c_ref, w_ref, b_ref, o_ref):
    act = jax.nn.silu(c_ref[...])
    o_ref[...] = jnp.dot(act, w_ref[...], precision=lax.Precision.HIGHEST,
                         preferred_element_type=F32) + b_ref[...]


def _modulations(c, c_ctx, w_mod, b_mod):
    depth, d, n = w_mod.shape
    nb = c.shape[0]
    rows = 2 * nb
    cc = jnp.zeros((rows, d), F32).at[:nb].set(c).at[nb].set(c_ctx)
    tn = n // 4
    return pl.pallas_call(
        _mod_kernel,
        grid=(depth, n // tn),
        in_specs=[
            pl.BlockSpec((rows, d), lambda i, j: (0, 0)),
            pl.BlockSpec((None, d, tn), lambda i, j: (i, 0, j)),
            pl.BlockSpec((None, 1, tn), lambda i, j: (i, 0, j)),
        ],
        out_specs=pl.BlockSpec((None, rows, tn), lambda i, j: (i, 0, j)),
        out_shape=jax.ShapeDtypeStruct((depth, rows, n), F32),
        compiler_params=_cparams(("parallel", "parallel")),
        name="adaln_mod",
    )(cc, w_mod, b_mod[:, None, :])


def _block_diag(w):
    h, bw, _ = w.shape
    eye = jnp.eye(h, dtype=w.dtype)
    return (eye[:, None, :, None] * w[:, :, None, :]).reshape(h * bw, h * bw)


def _dup_kv_columns(w):
    nq = N_HEADS * HEAD_DIM
    d = w.shape[0]
    kv = w[:, nq:].reshape(d, 2 * N_KV, 1, HEAD_DIM)
    kv = jnp.broadcast_to(kv, (d, 2 * N_KV, LANES // HEAD_DIM, HEAD_DIM)).reshape(d, -1)
    return jnp.concatenate([w[:, :nq], kv], axis=1)


def _split_mod(m):
    return [t[:, None, :] for t in jnp.split(m, 6, axis=-1)]


def _rope_tables(s):
    half = HEAD_DIM // 2
    nf = half // 2
    inv = jnp.power(ROPE_BASE, -jnp.arange(nf, dtype=F32) / nf)
    pos = jnp.arange(s, dtype=jnp.int32)
    row = (pos // GRID_W).astype(F32)[:, None] * inv
    col = (pos % GRID_W).astype(F32)[:, None] * inv
    cos = jnp.concatenate([jnp.cos(row), jnp.cos(row), jnp.cos(col), jnp.cos(col)], axis=-1)
    sin = jnp.concatenate([-jnp.sin(row), jnp.sin(row), -jnp.sin(col), jnp.sin(col)], axis=-1)
    return jnp.tile(cos, (1, LANES // HEAD_DIM)), jnp.tile(sin, (1, LANES // HEAD_DIM))


def kernel(x, c, ctx, c_ctx, norm1, norm2, w_mod, b_mod, ffn_w1, ffn_w3, ffn_w2, ab_w_in, hy_conv_w, hy_conv_b, hy_f_w1, hy_f_b1, hy_f_w2, hy_f_b2, hy_f_w3, hy_f_freq, hy_bias, lru_conv_w, lru_conv_b, lru_w_a, lru_b_a, lru_w_i, lru_b_i, lru_lam, ab_w_out, at_w_qkv, at_q_gain, at_k_gain, at_sink, at_w_o):
    nb, s, d = x.shape
    cl = ctx.shape[1]
    bf = lambda a: a.astype(BF16)

    mods = _modulations(c, c_ctx, w_mod, b_mod)
    sh1x, sc1x, g1x, sh2x, sc2x, g2x = _split_mod(mods[0, :nb])
    sh1c, sc1c, g1c, sh2c, sc2c, g2c = _split_mod(jnp.broadcast_to(mods[0, nb:nb + 1], (nb, mods.shape[-1])))
    gain1 = norm1[0][None]
    gain2 = norm2[0][None]
    w_in = bf(ab_w_in[0])
    px = _proj(x, gain1, sc1x, sh1x, w_in, 512)
    pc = _proj(ctx, gain1, sc1c, sh1c, w_in, cl)

    mlp = (hy_f_w1[0], hy_f_b1[0], hy_f_w2[0], hy_f_b2[0], hy_f_freq[0])
    fc, fi = _ctx_dft_tables(cl)
    kf_x = _filter_spec_latent(_filter_mlp(s, *mlp), hy_f_w3[0], hy_bias[0])
    kf_c = _filter_spec_ctx(_filter_mlp(cl, *mlp), hy_f_w3[0], hy_bias[0], fc)
    zx = _hyena_latent(px, hy_conv_w[0], hy_conv_b[0], kf_x)
    zc = _hyena_ctx(pc, hy_conv_w[0], hy_conv_b[0], kf_c, fc, fi)

    lru_col = 3 * HY_W // LRU_W
    lcb = lru_conv_b[0][None]
    h0 = jnp.zeros((nb, LRU_W), F32)
    hx, hc = [], []
    for dr, reverse in enumerate((False, True)):
        wg = bf(jnp.concatenate([_block_diag(lru_w_a[0, dr]), _block_diag(lru_w_i[0, dr])], axis=1))
        bg = jnp.concatenate([lru_b_a[0, dr], lru_b_i[0, dr]])[None]
        lam = lru_lam[0, dr][None]
        hcs, h_end = _lru(pc, lru_col, lru_conv_w[0], lcb, wg, bg, lam, h0, reverse, cl)
        hxs, _ = _lru(px, lru_col, lru_conv_w[0], lcb, wg, bg, lam, h_end, reverse, 256)
        hx.append(hxs)
        hc.append(hcs)

    w_out = bf(ab_w_out[0])
    w1, w3, w2 = bf(ffn_w1[0]), bf(ffn_w3[0]), bf(ffn_w2[0])
    x = _mix_ffn(x, (zx, hx[0], hx[1], px), w_out, g1x, gain2, sc2x, sh2x, g2x, w1, w3, w2, 512, True)
    ctx = _mix_ffn(ctx, (zc, hc[0], hc[1], pc), w_out, g1c, gain2, sc2c, sh2c, g2c, w1, w3, w2, cl, True)

    sh1x, sc1x, g1x, sh2x, sc2x, g2x = _split_mod(mods[1, :nb])
    sh1c, sc1c = _split_mod(jnp.broadcast_to(mods[1, nb:nb + 1], (nb, mods.shape[-1])))[:2]
    gain1 = norm1[1][None]
    gain2 = norm2[1][None]
    w_qkv = bf(_dup_kv_columns(at_w_qkv[0]))
    qg =jnp.tile(at_q_gain[0], LANES // HEAD_DIM)[None]
    kg = jnp.tile(at_k_gain[0], LANES // HEAD_DIM)[None]
    cos, sin = _rope_tables(s)
    q, k, v = _qkv(x, gain1, sc1x, sh1x, w_qkv, qg, kg, cos, sin, 512)
    _, kx, vx = _qkv(ctx, gain1, sc1c, sh1c, w_qkv, qg, kg,
                     jnp.ones((cl, LANES), F32), jnp.zeros((cl, LANES), F32), cl)
    o = _attention(q, k, v, kx, vx, at_sink[0], WINDOW)
    w1, w3, w2 = bf(ffn_w1[1]), bf(ffn_w3[1]), bf(ffn_w2[1])
    return _mix_ffn(x, (o,), bf(at_w_o[0]), g1x, gain2, sc2x, sh2x, g2x, w1, w3, w2, 512, False)
```

```python
import functools
import math

import numpy as np
import jax
import jax.numpy as jnp
from jax import lax
from jax.experimental import pallas as pl
from jax.experimental.pallas import tpu as pltpu

F32 = jnp.float32
BF16 = jnp.bfloat16

EPS = 1e-6
D_MODEL = 1024
GRID_W = 64
HY_W = 512
HY_ORDER = 2
HY_BANDS = 16
HY_MOD_SHIFT = 0.05
HY_FAST_DECAY = 0.3
HY_SLOW_DECAY = 1.5
HY_DECAY_TARGET = 1e-2
LRU_W = 512
LRU_HEADS = 8
LRU_C = 8.0
HEAD_DIM = 64
N_HEADS = 16
N_KV = 4
GROUP = N_HEADS // N_KV
WINDOW = 128
ROPE_BASE = 10000.0
NEG_INF = -1e30

VMEM_LIMIT_BYTES = 58 * 1024 * 1024
LANES = 128

FFT_R = 128
FFT_H = FFT_R // 2
FFT_K1 = FFT_R // 2 + 1
FFT_K1P = 72
FFT_NG = 4
CONV_HALO = 16
FFN_HIDDEN_CHUNK = 512
FFN_OUT_CHUNK = 256
FILT_EMB_PAD = 40
U_PITCH = FFT_R + 8
A_PITCH = 2 * FFT_R + 8


def _cparams(sem):
    return pltpu.CompilerParams(dimension_semantics=sem, vmem_limit_bytes=VMEM_LIMIT_BYTES)


def _resident(block_shape, index_map):
    return pl.BlockSpec(block_shape, index_map, pipeline_mode=pl.Buffered(1))


def _norm_mod(x, gain, scale, shift):
    y = x * lax.rsqrt(jnp.mean(x * x, axis=-1, keepdims=True) + EPS)
    return (y * gain) * (1.0 + scale) + shift


def _proj_kernel(x_ref, gain_ref, sc_ref, sh_ref, w_ref, o_ref, *, n_chunk):
    h = _norm_mod(x_ref[...], gain_ref[...], sc_ref[...], sh_ref[...]).astype(BF16)
    n_out = o_ref.shape[-1]
    for n0 in range(0, n_out, n_chunk):
        o_ref[:, n0:n0 + n_chunk] = jnp.dot(
            h, w_ref[:, n0:n0 + n_chunk], preferred_element_type=F32).astype(o_ref.dtype)


def _proj(x, gain, sc, sh, w, tm):
    b, s, d = x.shape
    n = w.shape[1]
    return pl.pallas_call(
        functools.partial(_proj_kernel, n_chunk=512),
        grid=(b, s // tm),
        in_specs=[
            pl.BlockSpec((None, tm, d), lambda i, j: (i, j, 0)),
            pl.BlockSpec((1, d), lambda i, j: (0, 0)),
            pl.BlockSpec((None, 1, d), lambda i, j: (i, 0, 0)),
            pl.BlockSpec((None, 1, d), lambda i, j: (i, 0, 0)),
            _resident((d, n), lambda i, j: (0, 0)),
        ],
        out_specs=pl.BlockSpec((None, tm, n), lambda i, j: (i, j, 0)),
        out_shape=jax.ShapeDtypeStruct((b, s, n), BF16),
        compiler_params=_cparams(("parallel", "parallel")),
        name="proj_in",
    )(x, gain, sc, sh, w)


def _qkv_kernel(x_ref, gain_ref, sc_ref, sh_ref, w_ref, hm_ref, cq_ref, sq_ref, ck_ref, sk_ref,
                q_ref, k_ref, v_ref):
    tm = x_ref.shape[0]
    nq = q_ref.shape[-1]
    nkv = k_ref.shape[-1]
    wide = 2 * LANES
    h = _norm_mod(x_ref[...], gain_ref[...], sc_ref[...], sh_ref[...]).astype(BF16)
    hm = hm_ref[...]
    lane = lax.broadcasted_iota(jnp.int32, (tm, wide), 1)
    first_half = (lane % 32) < 16

    def norm_rope(y, cos_g, sin_g):
        inv = lax.rsqrt(jnp.dot((y * y).astype(BF16), hm, preferred_element_type=F32) + EPS)
        partner = jnp.where(first_half, pltpu.roll(y, wide - 16, 1), pltpu.roll(y, 16, 1))
        return (y * cos_g + partner * sin_g) * inv

    def tables(c_ref, s_ref):
        c, s = c_ref[...], s_ref[...]
        return jnp.concatenate([c, c], axis=1), jnp.concatenate([s, s], axis=1)

    cq, sq = tables(cq_ref, sq_ref)
    for n0 in range(0, nq, 512):
        y = jnp.dot(h, w_ref[:, n0:n0 + 512], preferred_element_type=F32)
        for j in range(512 // wide):
            q_ref[:, n0 + j * wide:n0 + (j + 1) * wide] = norm_rope(
                y[:, j * wide:(j + 1) * wide], cq, sq).astype(BF16)
    ck, sk = tables(ck_ref, sk_ref)
    y = jnp.dot(h, w_ref[:, nq:nq + nkv], preferred_element_type=F32)
    for j in range(nkv // wide):
        k_ref[:, j * wide:(j + 1) * wide] = norm_rope(y[:, j * wide:(j + 1) * wide], ck, sk).astype(BF16)
    v_ref[...] = jnp.dot(h, w_ref[:, nq + nkv:], preferred_element_type=F32).astype(BF16)


def _head_mean_matrix():
    head = np.arange(2 * LANES) // HEAD_DIM
    return jnp.asarray((head[:, None] == head[None, :]) / HEAD_DIM, F32).astype(BF16)


def _qkv(x, gain, sc, sh, w, cq, sq, ck, sk, tm):
    b, s, d = x.shape
    nq = N_HEADS * HEAD_DIM
    nkv = (w.shape[1] - nq) // 2
    tab = pl.BlockSpec((tm, LANES), lambda i, j: (j, 0))
    return pl.pallas_call(
        _qkv_kernel,
        grid=(b, s // tm),
        in_specs=[
            pl.BlockSpec((None, tm, d), lambda i, j: (i, j, 0)),
            pl.BlockSpec((1, d), lambda i, j: (0, 0)),
            pl.BlockSpec((None, 1, d), lambda i, j: (i, 0, 0)),
            pl.BlockSpec((None, 1, d), lambda i, j: (i, 0, 0)),
            _resident((d, nq + 2 * nkv), lambda i, j: (0, 0)),
            pl.BlockSpec((2 * LANES, 2 * LANES), lambda i, j: (0, 0)),
            tab, tab, tab, tab,
        ],
        out_specs=[
            pl.BlockSpec((None, tm, nq), lambda i, j: (i, j, 0)),
            pl.BlockSpec((None, tm, nkv), lambda i, j: (i, j, 0)),
            pl.BlockSpec((None, tm, nkv), lambda i, j: (i, j, 0)),
        ],
        out_shape=[
            jax.ShapeDtypeStruct((b, s, nq), BF16),
            jax.ShapeDtypeStruct((b, s, nkv), BF16),
            jax.ShapeDtypeStruct((b, s, nkv), BF16),
        ],
        compiler_params=_cparams(("parallel", "parallel")),
        name="qkv",
    )(x, gain, sc, sh, w, _head_mean_matrix(), cq, sq, ck, sk)


def _attn_kernel(sink_ref, bias_ref, q_ref, kp_ref, kc_ref, kn_ref, vp_ref, vc_ref, vn_ref,
                 kx_ref, vx_ref, o_ref):
    blk = q_ref.shape[0]
    n_loc = 3 * blk
    n_keys = n_loc + kx_ref.shape[0]
    bias = bias_ref[...]
    lane = lax.broadcasted_iota(jnp.int32, (blk, LANES), 1)
    low = lane < HEAD_DIM
    keep_low = jnp.where(low, 1.0, 0.0).astype(BF16)
    keep_high = jnp.where(low, 0.0, 1.0).astype(BF16)
    ones = jnp.ones((n_keys, LANES), BF16)
    nt = (((1,), (1,)), ((), ()))
    for g in range(N_KV):
        gl = slice(g * LANES, (g + 1) * LANES)
        k_all = jnp.concatenate([kp_ref[:, gl], kc_ref[:, gl], kn_ref[:, gl], kx_ref[:, gl]], axis=0)
        v_all = jnp.concatenate([vp_ref[:, gl], vc_ref[:, gl], vn_ref[:, gl], vx_ref[:, gl]], axis=0)
        v_aug = jnp.concatenate([v_all, ones], axis=1)
        q4 = jnp.concatenate(
            [q_ref[:, ((GROUP * g + j) // 2) * LANES:((GROUP * g + j) // 2 + 1) * LANES]
             * (keep_low if j % 2 == 0 else keep_high) for j in range(GROUP)], axis=0)
        s = lax.dot_general(q4, k_all, nt, preferred_element_type=F32)
        s_loc = s[:, :n_loc] + bias
        s_ctx = s[:, n_loc:]
        s_sink = jnp.concatenate(
            [jnp.full((blk, 1), sink_ref[GROUP * g + j], F32) for j in range(GROUP)], axis=0)
        m = jnp.maximum(jnp.maximum(jnp.max(s_loc, axis=-1, keepdims=True),
                                    jnp.max(s_ctx, axis=-1, keepdims=True)), s_sink)
        p = jnp.exp(jnp.concatenate([s_loc - m, s_ctx - m], axis=1).astype(BF16))
        pv = jnp.dot(p, v_aug, preferred_element_type=F32)
        o4 = pv[:, :LANES] / (pv[:, LANES:] + jnp.exp(s_sink - m))
        for pair in range(GROUP // 2):
            slab = jnp.where(low, o4[2 * pair * blk:(2 * pair + 1) * blk],
                             o4[(2 * pair + 1) * blk:(2 * pair + 2) * blk])
            col = (GROUP // 2) * g + pair
            o_ref[:, col * LANES:(col + 1) * LANES] = slab.astype(BF16)


def _attention(q, k, v, kx, vx, sink, blk):
    b, s, nq = q.shape
    nkv = k.shape[-1]
    cx = kx.shape[1]
    nb = s // blk
    qspec = pl.BlockSpec((None, blk, nq), lambda i, j, *_: (i, j, 0))
    prev = pl.BlockSpec((None, blk, nkv), lambda i, j, *_: (i, jnp.maximum(j - 1, 0), 0))
    cur = pl.BlockSpec((None, blk, nkv), lambda i, j, *_: (i, j, 0))
    nxt = pl.BlockSpec((None, blk, nkv), lambda i, j, *_: (i, jnp.minimum(j + 1, nb - 1), 0))
    cxs = pl.BlockSpec((None, cx, nkv), lambda i, j, *_: (i, 0, 0))
    r = np.arange(GROUP * blk)[:, None] % blk
    c = np.arange(3 * blk)[None, :]
    band = np.abs(c - blk - r) <= WINDOW
    variants = [band & (c >= blk), band, band & (c < 2 * blk)]
    bias = jnp.asarray(np.where(np.stack(variants), 0.0, NEG_INF), F32)
    bias_spec = pl.BlockSpec(
        (None, GROUP * blk, 3 * blk),
        lambda i, j, *_: (jnp.where(j == 0, 0, jnp.where(j == nb - 1, 2, 1)), 0, 0))
    return pl.pallas_call(
        _attn_kernel,
        grid_spec=pltpu.PrefetchScalarGridSpec(
            num_scalar_prefetch=1,
            grid=(b, nb),
            in_specs=[bias_spec, qspec, prev, cur, nxt, prev, cur, nxt, cxs, cxs],
            out_specs=pl.BlockSpec((None, blk, nq), lambda i, j, *_: (i, j, 0)),
        ),
        out_shape=jax.ShapeDtypeStruct((b, s, nq), BF16),
        compiler_params=_cparams(("parallel", "parallel")),
        name="window_attn",
    )(sink, bias, q, k, k, k, v, v, v, kx, vx)


def _lru_kernel(prev_ref, cur_ref, next_ref, cw_ref, cb_ref, wg_ref, bg_ref, lam_ref, h0_ref,
                h_ref, hend_ref, a_scr, b_scr, carry, *, reverse, n_chunks):
    nb, t, w = cur_ref.shape
    pitch = t + 8
    n_lane_groups = w // LANES
    i = pl.program_id(0)
    chunk = (n_chunks - 1 - i) if reverse else i

    @pl.when(i == 0)
    def _():
        carry[...] = h0_ref[...]

    lam = lam_ref[...]
    softplus_neg_lam = jnp.maximum(-lam, 0.0) + jnp.log1p(jnp.exp(-jnp.abs(lam)))
    cw = cw_ref[...]
    cb = cb_ref[...]
    bg = bg_ref[...]
    row = lax.broadcasted_iota(jnp.int32, (t, w), 0)
    has_prev = chunk > 0
    has_next = chunk < n_chunks - 1
    for b in range(nb):
        cur = cur_ref[b].astype(F32)
        pv = prev_ref[b].astype(F32)
        nx = next_ref[b].astype(F32)
        pm2 = jnp.where(has_prev, pv[14:15], 0.0)
        pm1 = jnp.where(has_prev, pv[15:16], 0.0)
        nx0 = jnp.where(has_next, nx[0:1], 0.0)
        xm1 = jnp.where(row == 0, pm1, pltpu.roll(cur, 1, 0))
        xm2 = jnp.where(row == 0, pm2, jnp.where(row == 1, pm1, pltpu.roll(cur, 2, 0)))
        xp1 = jnp.where(row == t - 1, nx0, pltpu.roll(cur, t - 1, 0))
        x = cw[0:1] * xm2 + cw[1:2] * xm1 + cw[2:3] * cur + cw[3:4] * xp1 + cb
        g = jnp.dot(x.astype(BF16), wg_ref[...], preferred_element_type=F32) + bg
        r = 0.5 + 0.5 * jnp.tanh(0.5 * g[:, :w])
        gate_i = 0.5 + 0.5 * jnp.tanh(0.5 * g[:, w:])
        log_a = (-LRU_C) * r * softplus_neg_lam
        a = jnp.exp(log_a)
        bb = jnp.sqrt(1.0 - a * a) * (gate_i * x)
        for gi in range(n_lane_groups):
            a_scr[gi, pl.ds(b * pitch, t), :] = a[:, gi * LANES:(gi + 1) * LANES]
            b_scr[gi, pl.ds(b * pitch, t), :] = bb[:, gi * LANES:(gi + 1) * LANES]

    def step(s, hs):
        tt = (t - 1 - s) if reverse else s
        out = []
        for gi in range(n_lane_groups):
            a_t = a_scr[gi, pl.ds(tt, nb, stride=pitch), :]
            b_t = b_scr[gi, pl.ds(tt, nb, stride=pitch), :]
            h_new = a_t * hs[gi] + b_t
            b_scr[gi, pl.ds(tt, nb, stride=pitch), :] = h_new
            out.append(h_new)
        return tuple(out)

    h_init = tuple(carry[:, gi * LANES:(gi + 1) * LANES] for gi in range(n_lane_groups))
    h_fin = lax.fori_loop(0, t, step, h_init, unroll=8)
    for gi in range(n_lane_groups):
        carry[:, gi * LANES:(gi + 1) * LANES] = h_fin[gi]
        hend_ref[:, gi * LANES:(gi + 1) * LANES] = h_fin[gi]
    for b in range(nb):
        for gi in range(n_lane_groups):
            h_ref[b, :, gi * LANES:(gi + 1) * LANES] = b_scr[gi, pl.ds(b * pitch, t), :]


def _lru(p, col_block, cw, cb, wg, bg, lam, h0, reverse, t):
    nb, s, _ = p.shape
    w = LRU_W
    n_chunks = s // t
    halo = 16
    tb = t // halo
    last_halo = s // halo - 1
    if reverse:
        cidx = lambda i: n_chunks - 1 - i
    else:
        cidx = lambda i: i
    kern = functools.partial(_lru_kernel, reverse=reverse, n_chunks=n_chunks)
    return pl.pallas_call(
        kern,
        grid=(n_chunks,),
        in_specs=[
            pl.BlockSpec((nb, halo, w), lambda i: (0, jnp.maximum(cidx(i) * tb - 1, 0), col_block)),
            pl.BlockSpec((nb, t, w), lambda i: (0, cidx(i), col_block)),
            pl.BlockSpec((nb, halo, w), lambda i: (0, jnp.minimum((cidx(i) + 1) * tb, last_halo), col_block)),
            pl.BlockSpec((4, w), lambda i: (0, 0)),
            pl.BlockSpec((1, w), lambda i: (0, 0)),
            _resident((w, 2 * w), lambda i: (0, 0)),
            pl.BlockSpec((1, 2 * w), lambda i: (0, 0)),
            pl.BlockSpec((1, w), lambda i: (0, 0)),
            pl.BlockSpec((nb, w), lambda i: (0, 0)),
        ],
        out_specs=[
            pl.BlockSpec((nb, t, w), lambda i: (0, cidx(i), 0)),
            pl.BlockSpec((nb, w), lambda i: (0, 0)),
        ],
        out_shape=[
            jax.ShapeDtypeStruct((nb, s, w), F32),
            jax.ShapeDtypeStruct((nb, w), F32),
        ],
        scratch_shapes=[
            pltpu.VMEM((w // LANES, nb * (t + 8), LANES), F32),
            pltpu.VMEM((w // LANES, nb * (t + 8), LANES), F32),
            pltpu.VMEM((nb, w), F32),
        ],
        compiler_params=_cparams(("arbitrary",)),
        name="rglru_rev" if reverse else "rglru_fwd",
    )(p, p, p, cw, cb, wg, bg, lam, h0)


def _conv3_chunk(ref, n1, n_chunks, w, b):
    rows = FFT_R
    base = pl.multiple_of(n1 * rows, rows)
    cur = ref[pl.ds(base, rows), :].astype(F32)
    pbase = pl.multiple_of(jnp.maximum(base - CONV_HALO, 0), CONV_HALO)
    nbase = pl.multiple_of(jnp.minimum(base + rows, (n_chunks - 1) * rows), CONV_HALO)
    prev = jnp.where(n1 > 0, ref[pl.ds(pbase, CONV_HALO), :].astype(F32)[CONV_HALO - 1:CONV_HALO], 0.0)
    nxt = jnp.where(n1 < n_chunks - 1, ref[pl.ds(nbase, CONV_HALO), :].astype(F32)[0:1], 0.0)
    row = lax.broadcasted_iota(jnp.int32, cur.shape, 0)
    xm1 = jnp.where(row == 0, prev, pltpu.roll(cur, 1, 0))
    xp1 = jnp.where(row == rows - 1, nxt, pltpu.roll(cur, rows - 1, 0))
    return w[0:1] * xm1 + w[1:2] * cur + w[2:3] * xp1 + b


def _fwd_level1(u_scr, a_scr, f1):
    def body(grp, c):
        n2 = grp * FFT_NG
        xs = jnp.concatenate(
            [u_scr[pl.ds(n2 + j, FFT_H, stride=U_PITCH), :] for j in range(FFT_NG)], axis=1).astype(BF16)
        a = jnp.dot(f1, xs, preferred_element_type=F32)
        for j in range(FFT_NG):
            lanes = slice(j * LANES, (j + 1) * LANES)
            a_scr[pl.ds(n2 + j, FFT_K1P, stride=A_PITCH), :] = a[:FFT_K1P, lanes]
            a_scr[pl.ds(FFT_R + n2 + j, FFT_K1P, stride=A_PITCH), :] = a[FFT_K1P:, lanes]
        return c

    lax.fori_loop(0, FFT_R // FFT_NG, body, 0, unroll=2)


def _hyena_kernel(v_ref, x1_ref, x2_ref, cwv_ref, cw1_ref, cw2_ref, cbv_ref, cb1_ref, cb2_ref,
                  f1_ref, g_ref, f1i_ref, kf_ref, o_ref, u_scr, a_scr):
    n_chunks = FFT_H
    cwv, cbv = cwv_ref[...], cbv_ref[...]

    def load_v(n1, c):
        u_scr[pl.ds(pl.multiple_of(n1 * U_PITCH, 8), FFT_R), :] = _conv3_chunk(v_ref, n1, n_chunks, cwv, cbv)
        return c

    lax.fori_loop(0, n_chunks, load_v, 0)
    f1 = f1_ref[...]
    f1i = f1i_ref[...]
    tn = (((0,), (0,)), ((), ()))

    for order, (gate_ref, cw_ref, cb_ref) in enumerate(((x1_ref, cw1_ref, cb1_ref), (x2_ref, cw2_ref, cb2_ref))):
        _fwd_level1(u_scr, a_scr, f1)

        def mid(k1, c):
            base = pl.multiple_of(k1 * A_PITCH, 8)
            g = g_ref[k1]
            ak = a_scr[pl.ds(base, 2 * FFT_R), :].astype(BF16)
            xk = jnp.dot(g, ak, preferred_element_type=F32)
            kf = kf_ref[order, k1].astype(F32)
            xr, xi = xk[:FFT_R], xk[FFT_R:]
            kr, ki = kf[:FFT_R], kf[FFT_R:]
            y = jnp.concatenate([xr * kr - xi * ki, xr * ki + xi * kr], axis=0).astype(BF16)
            a_scr[pl.ds(base, 2 * FFT_R), :] = lax.dot_general(g, y, tn, preferred_element_type=F32)
            return c

        lax.fori_loop(0, FFT_K1, mid, 0, unroll=5)

        def inv2(grp, c):
            n2 = grp * FFT_NG
            bn = jnp.concatenate(
                [jnp.concatenate([a_scr[pl.ds(n2 + j, FFT_K1P, stride=A_PITCH), :],
                                  a_scr[pl.ds(FFT_R + n2 + j, FFT_K1P, stride=A_PITCH), :]], axis=0)
                 for j in range(FFT_NG)], axis=1).astype(BF16)
            y = jnp.dot(f1i, bn, preferred_element_type=F32)
            for j in range(FFT_NG):
                u_scr[pl.ds(n2 + j, FFT_H, stride=U_PITCH), :] = y[:, j * LANES:(j + 1) * LANES]
            return c

        lax.fori_loop(0, FFT_R // FFT_NG, inv2, 0, unroll=2)

        cw, cb = cw_ref[...], cb_ref[...]

        def gate(n1, c):
            ub = pl.multiple_of(n1 * U_PITCH, 8)
            z = _conv3_chunk(gate_ref, n1, n_chunks, cw, cb) * u_scr[pl.ds(ub, FFT_R), :]
            if order == HY_ORDER - 1:
                o_ref[pl.ds(pl.multiple_of(n1 * FFT_R, FFT_R), FFT_R), :] = z.astype(o_ref.dtype)
            else:
                u_scr[pl.ds(ub, FFT_R), :] = z
            return c

        lax.fori_loop(0, n_chunks, gate, 0)


def _dft_tables():
    n = FFT_R * FFT_R
    k1 = np.arange(FFT_K1)[:, None]
    n1 = np.arange(FFT_H)[None, :]
    th = 2.0 * np.pi * k1 * n1 / FFT_R
    f1 = np.zeros((2 * FFT_K1P, FFT_H), np.float64)
    f1[:FFT_K1] = np.cos(th)
    f1[FFT_K1P:FFT_K1P + FFT_K1] = -np.sin(th)
    wgt = np.full((FFT_K1, 1), 2.0)
    wgt[0] = 1.0
    wgt[-1] = 1.0
    f1i = np.zeros((FFT_H, 2 * FFT_K1P), np.float64)
    f1i[:, :FFT_K1] = (wgt * np.cos(th)).T / n
    f1i[:, FFT_K1P:FFT_K1P + FFT_K1] = (-wgt * np.sin(th)).T / n
    kk = np.arange(FFT_K1)[:, None, None] + FFT_R * np.arange(FFT_R)[None, :, None]
    n2 = np.arange(FFT_R)[None, None, :]
    ph = 2.0 * np.pi * ((kk * n2) % n) / n
    gr, gi = np.cos(ph), -np.sin(ph)
    g = np.concatenate([np.concatenate([gr, -gi], axis=2), np.concatenate([gi, gr], axis=2)], axis=1)
    return (jnp.asarray(f1, F32).astype(BF16), jnp.asarray(g, F32).astype(BF16),
            jnp.asarray(f1i, F32).astype(BF16))


def _hyena_latent(p, conv_w, conv_b, kf):
    b, s, _ = p.shape
    assert s == FFT_H * FFT_R
    cb_ = LANES
    ncb = HY_W // cb_
    f1, g, f1i = _dft_tables()
    conv_b2 = conv_b.reshape(1, -1)
    seq = lambda off: pl.BlockSpec((None, s, cb_), lambda c, i: (i, 0, off + c))
    cws = lambda off: pl.BlockSpec((3, cb_), lambda c, i: (0, off + c))
    cbs = lambda off: pl.BlockSpec((1, cb_), lambda c, i: (0, off + c))
    return pl.pallas_call(
        _hyena_kernel,
        grid=(ncb, b),
        in_specs=[
            seq(0), seq(ncb), seq(2 * ncb),
            cws(0), cws(ncb), cws(2 * ncb),
            cbs(0), cbs(ncb), cbs(2 * ncb),
            _resident(f1.shape, lambda c, i: (0, 0)),
            _resident(g.shape, lambda c, i: (0, 0, 0)),
            _resident(f1i.shape, lambda c, i: (0, 0)),
            _resident((HY_ORDER, FFT_K1, 2 * FFT_R, cb_), lambda c, i: (0, 0, 0, c)),
        ],
        out_specs=pl.BlockSpec((None, s, cb_), lambda c, i: (i, 0, c)),
        out_shape=jax.ShapeDtypeStruct((b, s, HY_W), BF16),
        scratch_shapes=[
            pltpu.VMEM((FFT_H * U_PITCH, cb_), F32),
            pltpu.VMEM((FFT_K1P * A_PITCH, cb_), F32),
        ],
        compiler_params=_cparams(("arbitrary", "arbitrary")),
        name="hyena_latent",
    )(p, p, p, conv_w, conv_w, conv_w, conv_b2, conv_b2, conv_b2, f1, g, f1i, kf)


def _hyena_ctx_kernel(u_ref, cw_ref, cb_ref, fc_ref, fi_ref, kf_ref, o_ref):
    l = u_ref.shape[0]
    half = fc_ref.shape[0] // 2
    u = u_ref[...].astype(F32)
    row = lax.broadcasted_iota(jnp.int32, u.shape, 0)
    cw = cw_ref[...]
    um1 = jnp.where(row == 0, 0.0, pltpu.roll(u, 1, 0))
    up1 = jnp.where(row == l - 1, 0.0, pltpu.roll(u, l - 1, 0))
    uc = cw[0:1] * um1 + cw[1:2] * u + cw[2:3] * up1 + cb_ref[...]
    z = uc[:, :HY_W]
    hi = lax.Precision.HIGHEST
    for order in range(HY_ORDER):
        gate = uc[:, (order + 1) * HY_W:(order + 2) * HY_W]
        xf = jnp.dot(fc_ref[...], z, precision=hi, preferred_element_type=F32)
        kf = kf_ref[order]
        xr, xi = xf[:half], xf[half:]
        kr, ki = kf[:half], kf[half:]
        y = jnp.concatenate([xr * kr - xi * ki, xr * ki + xi * kr], axis=0)
        z = gate * jnp.dot(fi_ref[...], y, precision=hi, preferred_element_type=F32)
    o_ref[...] = z.astype(o_ref.dtype)


def _ctx_dft_tables(l):
    n = 2 * l
    half = l + 8
    k = np.arange(l + 1)[:, None]
    t = np.arange(l)[None, :]
    th = 2.0 * np.pi * ((k * t) % n) / n
    fc = np.zeros((2 * half, l), np.float64)
    fc[:l + 1] = np.cos(th)
    fc[half:half + l + 1] = -np.sin(th)
    wgt = np.full((l + 1, 1), 2.0)
    wgt[0] = 1.0
    wgt[-1] = 1.0
    fi = np.zeros((l, 2 * half), np.float64)
    fi[:, :l + 1] = (wgt * np.cos(th)).T / n
    fi[:, half:half + l + 1] = (-wgt * np.sin(th)).T / n
    return jnp.asarray(fc, F32), jnp.asarray(fi, F32)


def _hyena_ctx(p, conv_w, conv_b, kf, fc, fi):
    b, l, _ = p.shape
    half = fc.shape[0] // 2
    wtot = 3 * HY_W
    return pl.pallas_call(
        _hyena_ctx_kernel,
        grid=(b,),
        in_specs=[
            pl.BlockSpec((None, l, wtot), lambda i: (i, 0, 0)),
            pl.BlockSpec((3, wtot), lambda i: (0, 0)),
            pl.BlockSpec((1, wtot), lambda i: (0, 0)),
            pl.BlockSpec((2 * half, l), lambda i: (0, 0)),
            pl.BlockSpec((l, 2 * half), lambda i: (0, 0)),
            pl.BlockSpec((HY_ORDER, 2 * half, HY_W), lambda i: (0, 0, 0)),
        ],
        out_specs=pl.BlockSpec((None, l, HY_W), lambda i: (i, 0, 0)),
        out_shape=jax.ShapeDtypeStruct((b, l, HY_W), BF16),
        compiler_params=_cparams(("parallel",)),
        name="hyena_ctx",
    )(p, conv_w, conv_b.reshape(1, -1), fc, fi, kf)


def _mix_ffn_kernel(*refs, recurrent):
    if recurrent:
        (x_ref, z_ref, hf_ref, hr_ref, gx_ref, wo_ref, g1_ref, gain_ref, sc_ref, sh_ref, g2_ref,
         w1_ref, w3_ref, w2_ref, o_ref, x1_scr, h_scr, t_scr) = refs
    else:
        (x_ref, m_ref, wo_ref, g1_ref, gain_ref, sc_ref, sh_ref, g2_ref,
         w1_ref, w3_ref, w2_ref, o_ref, x1_scr, h_scr, t_scr) = refs
    if recurrent:
        hw = z_ref.shape[-1]
        rec = ((hf_ref[...] + hr_ref[...]) * jax.nn.gelu(gx_ref[...].astype(F32))).astype(BF16)
        mixed = (jnp.dot(z_ref[...], wo_ref[:hw, :], preferred_element_type=F32)
                 + jnp.dot(rec, wo_ref[hw:, :], preferred_element_type=F32))
    else:
        mixed = jnp.dot(m_ref[...], wo_ref[...], preferred_element_type=F32)
    x1 = x_ref[...] + g1_ref[...] * mixed
    x1_scr[...] = x1
    h_scr[...] = _norm_mod(x1, gain_ref[...], sc_ref[...], sh_ref[...]).astype(BF16)

    d, ff = w1_ref.shape
    for c0 in range(0, ff, FFN_HIDDEN_CHUNK):
        cols = slice(c0, min(c0 + FFN_HIDDEN_CHUNK, ff))
        h = h_scr[...]
        a = jnp.dot(h, w1_ref[:, cols], preferred_element_type=F32)
        bgate = jnp.dot(h, w3_ref[:, cols], preferred_element_type=F32)
        t_scr[:, cols] = (jax.nn.silu(a) * bgate).astype(BF16)
    for n0 in range(0, d, FFN_OUT_CHUNK):
        cols = slice(n0, n0 + FFN_OUT_CHUNK)
        y = jnp.dot(t_scr[...], w2_ref[:, cols], preferred_element_type=F32)
        o_ref[:, cols] = x1_scr[:, cols] + g2_ref[:, cols] * y


def _mix_ffn(x, mixer_inputs, wo, g1, gain, sc, sh, g2, w1, w3, w2, tm, recurrent):
    b, s, d = x.shape
    ff = w1.shape[1]
    row = lambda width, col=0: pl.BlockSpec((None, tm, width), lambda i, r: (i, r, col))
    per_b = pl.BlockSpec((None, 1, d), lambda i, r: (i, 0, 0))
    if recurrent:
        z, hf, hr, p = mixer_inputs
        mix_specs = [row(HY_W), row(LRU_W), row(LRU_W), row(LRU_W, (3 * HY_W + LRU_W) // LRU_W)]
        mix_args = [z, hf, hr, p]
    else:
        (m,) = mixer_inputs
        mix_specs = [row(d)]
        mix_args = [m]
    return pl.pallas_call(
        functools.partial(_mix_ffn_kernel, recurrent=recurrent),
        grid=(b, s // tm),
        in_specs=[row(d)] + mix_specs + [
            _resident((d, d), lambda i, r: (0, 0)),
            per_b,
            pl.BlockSpec((1, d), lambda i, r: (0, 0)),
            per_b, per_b, per_b,
            _resident((d, ff), lambda i, r: (0, 0)),
            _resident((d, ff), lambda i, r: (0, 0)),
            _resident((ff, d), lambda i, r: (0, 0)),
        ],
        out_specs=row(d),
        out_shape=jax.ShapeDtypeStruct((b, s, d), F32),
        scratch_shapes=[
            pltpu.VMEM((tm, d), F32),
            pltpu.VMEM((tm, d), BF16),
            pltpu.VMEM((tm, ff), BF16),
        ],
        compiler_params=_cparams(("parallel", "parallel")),
        name="mix_ffn_rec" if recurrent else "mix_ffn_attn",
    )(x, *mix_args, wo, g1, gain, sc, sh, g2, w1, w3, w2)


def _filter_features(l):
    t = jnp.linspace(0.0, 1.0, l, dtype=F32)[:, None]
    bands = jnp.linspace(1e-4, HY_BANDS - 1, HY_BANDS, dtype=F32)
    w = 2.0 * math.pi * jnp.arange(l, dtype=F32)[:, None] / l
    z = jnp.concatenate([t, jnp.cos(bands * w), -jnp.sin(bands * w)], axis=-1)
    return jnp.pad(z, ((0, 0), (0, FILT_EMB_PAD - z.shape[1])))


def _filter_mlp_kernel(z_ref, w1_ref, b1_ref, w2_ref, b2_ref, fr_ref, o_ref):
    hp = lax.Precision.HIGHEST
    fr = fr_ref[...]
    h = jnp.sin(fr * (jnp.dot(z_ref[...], w1_ref[...], precision=hp, preferred_element_type=F32) + b1_ref[...]))
    o_ref[...] = jnp.sin(fr * (jnp.dot(h, w2_ref[...], precision=hp, preferred_element_type=F32) + b2_ref[...]))


def _filter_mlp(l, w1, b1, w2, b2, freq):
    z = _filter_features(l)
    w1p = jnp.pad(w1, ((0, FILT_EMB_PAD - w1.shape[0]), (0, 0)))
    hid = w2.shape[0]
    return pl.pallas_call(
        _filter_mlp_kernel,
        out_shape=jax.ShapeDtypeStruct((l, hid), F32),
        compiler_params=pltpu.CompilerParams(vmem_limit_bytes=VMEM_LIMIT_BYTES),
        name="filter_mlp",
    )(z, w1p, b1[None], w2, b2[None], freq[None])


def _filter_decay():
    max_decay = math.log(HY_DECAY_TARGET) / HY_FAST_DECAY
    min_decay = math.log(HY_DECAY_TARGET) / HY_SLOW_DECAY
    return jnp.abs(jnp.linspace(min_decay, max_decay, HY_W, dtype=F32))[None]


def _filter_taps(h, t, w3f, w3b, decay, row0):
    hp = lax.Precision.HIGHEST
    window = jnp.exp(-t * decay) + HY_MOD_SHIFT
    hf = jnp.dot(h, w3f, precision=hp, preferred_element_type=F32) * window
    hb = jnp.dot(h, w3b, precision=hp, preferred_element_type=F32) * window
    row = lax.broadcasted_iota(jnp.int32, hb.shape, 0) + row0
    return hf, jnp.where(row == 0, 0.0, hb)


def _filter_spec_latent_kernel(h_ref, t_ref, w3f_ref, w3b_ref, dec_ref, bias_ref, f1_ref, g_ref,
                               o_ref, uf_scr, ub_scr, af_scr, ab_scr):
    w3f, w3b, dec = w3f_ref[...], w3b_ref[...], dec_ref[...]

    def load(n1, acc):
        base = pl.multiple_of(n1 * FFT_R, FFT_R)
        hf, hb = _filter_taps(h_ref[pl.ds(base, FFT_R), :], t_ref[pl.ds(base, FFT_R), :], w3f, w3b, dec, base)
        ub = pl.multiple_of(n1 * U_PITCH, 8)
        uf_scr[pl.ds(ub, FFT_R), :] = hf
        ub_scr[pl.ds(ub, FFT_R), :] = hb
        return acc + jnp.sum(jnp.abs(hf) + jnp.abs(hb), axis=0, keepdims=True)

    l1 = lax.fori_loop(0, FFT_H, load, jnp.zeros((1, LANES), F32))
    inv = 1.0 / l1
    bias = bias_ref[...]
    f1 = f1_ref[...]
    _fwd_level1(uf_scr, af_scr, f1)
    _fwd_level1(ub_scr, ab_scr, f1)

    def level2(k1, c):
        base = pl.multiple_of(k1 * A_PITCH, 8)
        g = g_ref[k1]
        xf = jnp.dot(g, af_scr[pl.ds(base, 2 * FFT_R), :].astype(BF16), preferred_element_type=F32)
        xb = jnp.dot(g, ab_scr[pl.ds(base, 2 * FFT_R), :].astype(BF16), preferred_element_type=F32)
        kr = (xf[:FFT_R] + xb[:FFT_R]) * inv + bias
        ki = (xf[FFT_R:] - xb[FFT_R:]) * inv
        o_ref[k1] = jnp.concatenate([kr, ki], axis=0).astype(o_ref.dtype)
        return c

    lax.fori_loop(0, FFT_K1, level2, 0, unroll=5)


def _filter_spec_latent(h, w3, bias):
    l, hid = h.shape
    assert l == FFT_H * FFT_R
    f1, g, _ = _dft_tables()
    ncb = HY_W // LANES
    t = jnp.linspace(0.0, 1.0, l, dtype=F32)[:, None]
    return pl.pallas_call(
        _filter_spec_latent_kernel,
        grid=(HY_ORDER * ncb,),
        in_specs=[
            pl.BlockSpec((l, hid), lambda i: (0, 0)),
            pl.BlockSpec((l, 1), lambda i: (0, 0)),
            pl.BlockSpec((hid, LANES), lambda i: (0, i)),
            pl.BlockSpec((hid, LANES), lambda i: (0, HY_ORDER * ncb + i)),
            pl.BlockSpec((1, LANES), lambda i: (0, i % ncb)),
            pl.BlockSpec((None, 1, LANES), lambda i: (i // ncb, 0, i % ncb)),
            _resident(f1.shape, lambda i: (0, 0)),
            _resident(g.shape, lambda i: (0, 0, 0)),
        ],
        out_specs=pl.BlockSpec((None, FFT_K1, 2 * FFT_R, LANES), lambda i: (i // ncb, 0, 0, i % ncb)),
        out_shape=jax.ShapeDtypeStruct((HY_ORDER, FFT_K1, 2 * FFT_R, HY_W), BF16),
        scratch_shapes=[
            pltpu.VMEM((FFT_H * U_PITCH, LANES), F32),
            pltpu.VMEM((FFT_H * U_PITCH, LANES), F32),
            pltpu.VMEM((FFT_K1P * A_PITCH, LANES), F32),
            pltpu.VMEM((FFT_K1P * A_PITCH, LANES), F32),
        ],
        compiler_params=_cparams(("arbitrary",)),
        name="filter_spec_latent",
    )(h, t, w3, w3, _filter_decay(), bias[:, None, :], f1, g)


def _filter_spec_ctx_kernel(h_ref, t_ref, w3f_ref, w3b_ref, dec_ref, bias_ref, fc_ref, o_ref):
    hp = lax.Precision.HIGHEST
    half = fc_ref.shape[0] // 2
    hf, hb = _filter_taps(h_ref[...], t_ref[...], w3f_ref[...], w3b_ref[...], dec_ref[...], 0)
    inv = 1.0 / jnp.sum(jnp.abs(hf) + jnp.abs(hb), axis=0, keepdims=True)
    xf = jnp.dot(fc_ref[...], hf, precision=hp, preferred_element_type=F32)
    xb = jnp.dot(fc_ref[...], hb, precision=hp, preferred_element_type=F32)
    kr = (xf[:half] + xb[:half]) * inv + bias_ref[...]
    ki = (xf[half:] - xb[half:]) * inv
    o_ref[...] = jnp.concatenate([kr, ki], axis=0)


def _filter_spec_ctx(h, w3, bias, fc):
    l, hid = h.shape
    ncb = HY_W // LANES
    t = jnp.linspace(0.0, 1.0, l, dtype=F32)[:, None]
    return pl.pallas_call(
        _filter_spec_ctx_kernel,
        grid=(HY_ORDER * ncb,),
        in_specs=[
            pl.BlockSpec((l, hid), lambda i: (0, 0)),
            pl.BlockSpec((l, 1), lambda i: (0, 0)),
            pl.BlockSpec((hid, LANES), lambda i: (0, i)),
            pl.BlockSpec((hid, LANES), lambda i: (0, HY_ORDER * ncb + i)),
            pl.BlockSpec((1, LANES), lambda i: (0, i % ncb)),
            pl.BlockSpec((None, 1, LANES), lambda i: (i // ncb, 0, i % ncb)),
            pl.BlockSpec(fc.shape, lambda i: (0, 0)),
        ],
        out_specs=pl.BlockSpec((None, fc.shape[0], LANES), lambda i: (i // ncb, 0, i % ncb)),
        out_shape=jax.ShapeDtypeStruct((HY_ORDER, fc.shape[0], HY_W), F32),
        compiler_params=_cparams(("parallel",)),
        name="filter_spec_ctx",
    )(h, t, w3, w3, _filter_decay(), bias[:, None, :], fc)


def _mod_kernel(c_ref, w_ref, b_ref, o_ref):
    act = jax.nn.silu(c_ref[...])
    o_ref[...] = jnp.dot(act, w_ref[...], precision=lax.Precision.HIGHEST,
                         preferred_element_type=F32) + b_ref[...]


def _modulations(c, c_ctx, w_mod, b_mod):
    depth, d, n = w_mod.shape
    nb = c.shape[0]
    rows = 2 * nb
    cc = jnp.zeros((rows, d), F32).at[:nb].set(c).at[nb].set(c_ctx)
    tn = n // 4
    return pl.pallas_call(
        _mod_kernel,
        grid=(depth, n // tn),
        in_specs=[
            pl.BlockSpec((rows, d), lambda i, j: (0, 0)),
            pl.BlockSpec((None, d, tn), lambda i, j: (i, 0, j)),
            pl.BlockSpec((None, 1, tn), lambda i, j: (i, 0, j)),
        ],
        out_specs=pl.BlockSpec((None, rows, tn), lambda i, j: (i, 0, j)),
        out_shape=jax.ShapeDtypeStruct((depth, rows, n), F32),
        compiler_params=_cparams(("parallel", "parallel")),
        name="adaln_mod",
    )(cc, w_mod, b_mod[:, None, :])


def _block_diag(w):
    h, bw, _ = w.shape
    eye = jnp.eye(h, dtype=w.dtype)
    return (eye[:, None, :, None] * w[:, :, None, :]).reshape(h * bw, h * bw)


def _dup_kv_columns(w):
    nq = N_HEADS * HEAD_DIM
    d = w.shape[0]
    kv = w[:, nq:].reshape(d, 2 * N_KV, 1, HEAD_DIM)
    kv = jnp.broadcast_to(kv, (d, 2 * N_KV, LANES // HEAD_DIM, HEAD_DIM)).reshape(d, -1)
    return jnp.concatenate([w[:, :nq], kv], axis=1)


def _split_mod(m):
    return [t[:, None, :] for t in jnp.split(m, 6, axis=-1)]


def _rope_tables(s):
    half = HEAD_DIM // 2
    nf = half // 2
    inv = jnp.power(ROPE_BASE, -jnp.arange(nf, dtype=F32) / nf)
    pos = jnp.arange(s, dtype=jnp.int32)
    row = (pos // GRID_W).astype(F32)[:, None] * inv
    col = (pos % GRID_W).astype(F32)[:, None] * inv
    cos = jnp.concatenate([jnp.cos(row), jnp.cos(row), jnp.cos(col), jnp.cos(col)], axis=-1)
    sin = jnp.concatenate([-jnp.sin(row), jnp.sin(row), -jnp.sin(col), jnp.sin(col)], axis=-1)
    return jnp.tile(cos, (1, LANES // HEAD_DIM)), jnp.tile(sin, (1, LANES // HEAD_DIM))


def _rope_gain_tables(cos, sin, gain, scale):
    g = jnp.tile(gain, LANES // HEAD_DIM)
    first_half = (jnp.arange(LANES) % 32) < 16
    g_partner = jnp.where(first_half, jnp.roll(g, -16), jnp.roll(g, 16))
    return cos * (g * scale)[None], sin * (g_partner * scale)[None]


def kernel(x, c, ctx, c_ctx, norm1, norm2, w_mod, b_mod, ffn_w1, ffn_w3, ffn_w2, ab_w_in, hy_conv_w, hy_conv_b, hy_f_w1, hy_f_b1, hy_f_w2, hy_f_b2, hy_f_w3, hy_f_freq, hy_bias, lru_conv_w, lru_conv_b, lru_w_a, lru_b_a, lru_w_i, lru_b_i, lru_lam, ab_w_out, at_w_qkv, at_q_gain, at_k_gain, at_sink, at_w_o):
    nb, s, d = x.shape
    cl = ctx.shape[1]
    bf = lambda a: a.astype(BF16)

    mods = _modulations(c, c_ctx, w_mod, b_mod)
    sh1x, sc1x, g1x, sh2x, sc2x, g2x = _split_mod(mods[0, :nb])
    sh1c, sc1c, g1c, sh2c, sc2c, g2c = _split_mod(jnp.broadcast_to(mods[0, nb:nb + 1], (nb, mods.shape[-1])))
    gain1 = norm1[0][None]
    gain2 = norm2[0][None]
    w_in = bf(ab_w_in[0])
    px = _proj(x, gain1, sc1x, sh1x, w_in, 512)
    pc = _proj(ctx, gain1, sc1c, sh1c, w_in, cl)

    mlp = (hy_f_w1[0], hy_f_b1[0], hy_f_w2[0], hy_f_b2[0], hy_f_freq[0])
    fc, fi = _ctx_dft_tables(cl)
    kf_x = _filter_spec_latent(_filter_mlp(s, *mlp), hy_f_w3[0], hy_bias[0])
    kf_c = _filter_spec_ctx(_filter_mlp(cl, *mlp), hy_f_w3[0], hy_bias[0], fc)
    zx = _hyena_latent(px, hy_conv_w[0], hy_conv_b[0], kf_x)
    zc = _hyena_ctx(pc, hy_conv_w[0], hy_conv_b[0], kf_c, fc, fi)

    lru_col = 3 * HY_W // LRU_W
    lcb = lru_conv_b[0][None]
    h0 = jnp.zeros((nb, LRU_W), F32)
    hx, hc = [], []
    for dr, reverse in enumerate((False, True)):
        wg = bf(jnp.concatenate([_block_diag(lru_w_a[0, dr]), _block_diag(lru_w_i[0, dr])], axis=1))
        bg = jnp.concatenate([lru_b_a[0, dr], lru_b_i[0, dr]])[None]
        lam = lru_lam[0, dr][None]
        hcs, h_end = _lru(pc, lru_col, lru_conv_w[0], lcb, wg, bg, lam, h0, reverse, cl)
        hxs, _ = _lru(px, lru_col, lru_conv_w[0], lcb, wg, bg, lam, h_end, reverse, 256)
        hx.append(hxs)
        hc.append(hcs)

    w_out = bf(ab_w_out[0])
    w1, w3, w2 = bf(ffn_w1[0]), bf(ffn_w3[0]), bf(ffn_w2[0])
    x = _mix_ffn(x, (zx, hx[0], hx[1], px), w_out, g1x, gain2, sc2x, sh2x, g2x, w1, w3, w2, 512, True)
    ctx = _mix_ffn(ctx, (zc, hc[0], hc[1], pc), w_out, g1c, gain2, sc2c, sh2c, g2c, w1, w3, w2, cl, True)

    sh1x, sc1x, g1x, sh2x, sc2x, g2x = _split_mod(mods[1, :nb])
    sh1c, sc1c = _split_mod(jnp.broadcast_to(mods[1, nb:nb + 1], (nb, mods.shape[-1])))[:2]
    gain1 = norm1[1][None]
    gain2 = norm2[1][None]
    w_qkv = bf(_dup_kv_columns(at_w_qkv[0]))
    cos, sin = _rope_tables(s)
    q_scale = HEAD_DIM ** -0.5
    q, k, v = _qkv(x, gain1, sc1x, sh1x, w_qkv,
                   *_rope_gain_tables(cos, sin, at_q_gain[0], q_scale),
                   *_rope_gain_tables(cos, sin, at_k_gain[0], 1.0), 512)
    no_rot = (jnp.ones((cl, LANES), F32), jnp.zeros((cl, LANES), F32))
    _, kx, vx = _qkv(ctx, gain1, sc1c, sh1c, w_qkv,
                     *_rope_gain_tables(*no_rot, at_q_gain[0], q_scale),
                     *_rope_gain_tables(*no_rot, at_k_gain[0], 1.0), cl)
    o = _attention(q, k, v, kx, vx, at_sink[0], WINDOW)
    w1, w3, w2 = bf(ffn_w1[1]), bf(ffn_w3[1]), bf(ffn_w2[1])
    return _mix_ffn(x, (o,), bf(at_w_o[0]), g1x, gain2, sc2x, sh2x, g2x, w1, w3, w2, 512, False)
```

```python
import functools
import math

import numpy as np
import jax
import jax.numpy as jnp
from jax import lax
from jax.experimental import pallas as pl
from jax.experimental.pallas import tpu as pltpu

F32 = jnp.float32
BF16 = jnp.bfloat16

EPS = 1e-6
D_MODEL = 1024
GRID_W = 64
HY_W = 512
HY_ORDER = 2
HY_BANDS = 16
HY_MOD_SHIFT = 0.05
HY_FAST_DECAY = 0.3
HY_SLOW_DECAY = 1.5
HY_DECAY_TARGET = 1e-2
LRU_W = 512
LRU_HEADS = 8
LRU_C = 8.0
HEAD_DIM = 64
N_HEADS = 16
N_KV = 4
GROUP = N_HEADS // N_KV
WINDOW = 128
ROPE_BASE = 10000.0
NEG_INF = -1e30
TINY = 1e-30

VMEM_LIMIT_BYTES = 58 * 1024 * 1024
LANES = 128

FFT_R = 128
FFT_H = FFT_R // 2
FFT_K1 = FFT_R // 2 + 1
FFT_K1P = 72
FFT_NG = 4
CONV_HALO = 16
FFN_HIDDEN_CHUNK = 256
FFN_OUT_CHUNK = 256
FILT_EMB_PAD = 40
U_PITCH = FFT_R + 8
A_PITCH = 2 * FFT_R + 8


def _cparams(sem):
    return pltpu.CompilerParams(dimension_semantics=sem, vmem_limit_bytes=VMEM_LIMIT_BYTES)


def _resident(block_shape, index_map):
    return pl.BlockSpec(block_shape, index_map, pipeline_mode=pl.Buffered(1))


def _norm_mod(x, gain, scale, shift):
    y = x * lax.rsqrt(jnp.mean(x * x, axis=-1, keepdims=True) + EPS)
    return (y * gain) * (1.0 + scale) + shift


def _proj_kernel(x_ref, gain_ref, sc_ref, sh_ref, w_ref, o_ref, *, n_chunk):
    h = _norm_mod(x_ref[...], gain_ref[...], sc_ref[...], sh_ref[...]).astype(BF16)
    n_out = o_ref.shape[-1]
    for n0 in range(0, n_out, n_chunk):
        o_ref[:, n0:n0 + n_chunk] = jnp.dot(
            h, w_ref[:, n0:n0 + n_chunk], preferred_element_type=F32).astype(o_ref.dtype)


def _proj(x, gain, sc, sh, w, tm):
    b, s, d = x.shape
    n = w.shape[1]
    return pl.pallas_call(
        functools.partial(_proj_kernel, n_chunk=512),
        grid=(b, s // tm),
        in_specs=[
            pl.BlockSpec((None, tm, d), lambda i, j: (i, j, 0)),
            pl.BlockSpec((1, d), lambda i, j: (0, 0)),
            pl.BlockSpec((None, 1, d), lambda i, j: (i, 0, 0)),
            pl.BlockSpec((None, 1, d), lambda i, j: (i, 0, 0)),
            _resident((d, n), lambda i, j: (0, 0)),
        ],
        out_specs=pl.BlockSpec((None, tm, n), lambda i, j: (i, j, 0)),
        out_shape=jax.ShapeDtypeStruct((b, s, n), BF16),
        compiler_params=_cparams(("parallel", "parallel")),
        name="proj_in",
    )(x, gain, sc, sh, w)


def _qkv_kernel(x_ref, gain_ref, sc_ref, sh_ref, w_ref, hm_ref, cq_ref, sq_ref, ck_ref, sk_ref,
                q_ref, k_ref, v_ref):
    tm = x_ref.shape[0]
    nq = q_ref.shape[-1]
    nkv = k_ref.shape[-1]
    wide = 2 * LANES
    h = _norm_mod(x_ref[...], gain_ref[...], sc_ref[...], sh_ref[...]).astype(BF16)
    hm = hm_ref[...]
    lane = lax.broadcasted_iota(jnp.int32, (tm, wide), 1)
    first_half = (lane % 32) < 16

    def norm_rope(y, cos_g, sin_g):
        inv = lax.rsqrt(jnp.dot((y * y).astype(BF16), hm, preferred_element_type=F32) + EPS)
        partner = jnp.where(first_half, pltpu.roll(y, wide - 16, 1), pltpu.roll(y, 16, 1))
        return (y * cos_g + partner * sin_g) * inv

    def tables(c_ref, s_ref):
        c, s = c_ref[...], s_ref[...]
        return jnp.concatenate([c, c], axis=1), jnp.concatenate([s, s], axis=1)

    cq, sq = tables(cq_ref, sq_ref)
    for n0 in range(0, nq, 512):
        y = jnp.dot(h, w_ref[:, n0:n0 + 512], preferred_element_type=F32)
        for j in range(512 // wide):
            q_ref[:, n0 + j * wide:n0 + (j + 1) * wide] = norm_rope(
                y[:, j * wide:(j + 1) * wide], cq, sq).astype(BF16)
    ck, sk = tables(ck_ref, sk_ref)
    y = jnp.dot(h, w_ref[:, nq:nq + nkv], preferred_element_type=F32)
    for j in range(nkv // wide):
        k_ref[:, j * wide:(j + 1) * wide] = norm_rope(y[:, j * wide:(j + 1) * wide], ck, sk).astype(BF16)
    v_ref[...] = jnp.dot(h, w_ref[:, nq + nkv:], preferred_element_type=F32).astype(BF16)


def _head_mean_matrix():
    head = np.arange(2 * LANES) // HEAD_DIM
    return jnp.asarray((head[:, None] == head[None, :]) / HEAD_DIM, F32).astype(BF16)


def _qkv(x, gain, sc, sh, w, cq, sq, ck, sk, tm):
    b, s, d = x.shape
    nq = N_HEADS * HEAD_DIM
    nkv = (w.shape[1] - nq) // 2
    tab = pl.BlockSpec((tm, LANES), lambda i, j: (j, 0))
    return pl.pallas_call(
        _qkv_kernel,
        grid=(b, s // tm),
        in_specs=[
            pl.BlockSpec((None, tm, d), lambda i, j: (i, j, 0)),
            pl.BlockSpec((1, d), lambda i, j: (0, 0)),
            pl.BlockSpec((None, 1, d), lambda i, j: (i, 0, 0)),
            pl.BlockSpec((None, 1, d), lambda i, j: (i, 0, 0)),
            _resident((d, nq + 2 * nkv), lambda i, j: (0, 0)),
            pl.BlockSpec((2 * LANES, 2 * LANES), lambda i, j: (0, 0)),
            tab, tab, tab, tab,
        ],
        out_specs=[
            pl.BlockSpec((None, tm, nq), lambda i, j: (i, j, 0)),
            pl.BlockSpec((None, tm, nkv), lambda i, j: (i, j, 0)),
            pl.BlockSpec((None, tm, nkv), lambda i, j: (i, j, 0)),
        ],
        out_shape=[
            jax.ShapeDtypeStruct((b, s, nq), BF16),
            jax.ShapeDtypeStruct((b, s, nkv), BF16),
            jax.ShapeDtypeStruct((b, s, nkv), BF16),
        ],
        compiler_params=_cparams(("parallel", "parallel")),
        name="qkv",
    )(x, gain, sc, sh, w, _head_mean_matrix(), cq, sq, ck, sk)


def _attn_kernel(sink_ref, bias_ref, q_ref, kp_ref, kc_ref, kn_ref, vp_ref, vc_ref, vn_ref,
                 kx_ref, vx_ref, o_ref):
    blk = q_ref.shape[0]
    n_loc = 3 * blk
    n_keys = n_loc + kx_ref.shape[0]
    bias = bias_ref[...]
    lane = lax.broadcasted_iota(jnp.int32, (blk, LANES), 1)
    low = lane < HEAD_DIM
    keep_low = jnp.where(low, 1.0, 0.0).astype(BF16)
    keep_high = jnp.where(low, 0.0, 1.0).astype(BF16)
    ones = jnp.ones((n_keys, LANES), BF16)
    nt = (((1,), (1,)), ((), ()))
    for g in range(N_KV):
        gl = slice(g * LANES, (g + 1) * LANES)
        k_all = jnp.concatenate([kp_ref[:, gl], kc_ref[:, gl], kn_ref[:, gl], kx_ref[:, gl]], axis=0)
        v_all = jnp.concatenate([vp_ref[:, gl], vc_ref[:, gl], vn_ref[:, gl], vx_ref[:, gl]], axis=0)
        v_aug = jnp.concatenate([v_all, ones], axis=1)
        q4 = jnp.concatenate(
            [q_ref[:, ((GROUP * g + j) // 2) * LANES:((GROUP * g + j) // 2 + 1) * LANES]
             * (keep_low if j % 2 == 0 else keep_high) for j in range(GROUP)], axis=0)
        s = lax.dot_general(q4, k_all, nt, preferred_element_type=F32)
        s_loc = s[:, :n_loc] + bias
        s_ctx = s[:, n_loc:]
        s_sink = jnp.concatenate(
            [jnp.full((blk, 1), sink_ref[GROUP * g + j], F32) for j in range(GROUP)], axis=0)
        m = jnp.maximum(jnp.maximum(jnp.max(s_loc, axis=-1, keepdims=True),
                                    jnp.max(s_ctx, axis=-1, keepdims=True)), s_sink)
        p = jnp.exp(jnp.concatenate([s_loc - m, s_ctx - m], axis=1).astype(BF16))
        pv = jnp.dot(p, v_aug, preferred_element_type=F32)
        o4 = pv[:, :LANES] / (pv[:, LANES:] + jnp.exp(s_sink - m))
        for pair in range(GROUP // 2):
            slab = jnp.where(low, o4[2 * pair * blk:(2 * pair + 1) * blk],
                             o4[(2 * pair + 1) * blk:(2 * pair + 2) * blk])
            col = (GROUP // 2) * g + pair
            o_ref[:, col * LANES:(col + 1) * LANES] = slab.astype(BF16)


def _attention(q, k, v, kx, vx, sink, blk):
    b, s, nq = q.shape
    nkv = k.shape[-1]
    cx = kx.shape[1]
    nb = s // blk
    qspec = pl.BlockSpec((None, blk, nq), lambda i, j, *_: (i, j, 0))
    prev = pl.BlockSpec((None, blk, nkv), lambda i, j, *_: (i, jnp.maximum(j - 1, 0), 0))
    cur = pl.BlockSpec((None, blk, nkv), lambda i, j, *_: (i, j, 0))
    nxt = pl.BlockSpec((None, blk, nkv), lambda i, j, *_: (i, jnp.minimum(j + 1, nb - 1), 0))
    cxs = pl.BlockSpec((None, cx, nkv), lambda i, j, *_: (i, 0, 0))
    r = np.arange(GROUP * blk)[:, None] % blk
    c = np.arange(3 * blk)[None, :]
    band = np.abs(c - blk - r) <= WINDOW
    variants = [band & (c >= blk), band, band & (c < 2 * blk)]
    bias = jnp.asarray(np.where(np.stack(variants), 0.0, NEG_INF), F32)
    bias_spec = pl.BlockSpec(
        (None, GROUP * blk, 3 * blk),
        lambda i, j, *_: (jnp.where(j == 0, 0, jnp.where(j == nb - 1, 2, 1)), 0, 0))
    return pl.pallas_call(
        _attn_kernel,
        grid_spec=pltpu.PrefetchScalarGridSpec(
            num_scalar_prefetch=1,
            grid=(b, nb),
            in_specs=[bias_spec, qspec, prev, cur, nxt, prev, cur, nxt, cxs, cxs],
            out_specs=pl.BlockSpec((None, blk, nq), lambda i, j, *_: (i, j, 0)),
        ),
        out_shape=jax.ShapeDtypeStruct((b, s, nq), BF16),
        compiler_params=_cparams(("parallel", "parallel")),
        name="window_attn",
    )(sink, bias, q, k, k, k, v, v, v, kx, vx)


def _lru_kernel(prev_ref, cur_ref, next_ref, cw_ref, cb_ref, wg_ref, bg_ref, lam_ref, h0_ref,
                h_ref, hend_ref, a_scr, b_scr, carry, *, reverse, n_chunks):
    nb, t, w = cur_ref.shape
    pitch = t + 8
    n_lane_groups = w // LANES
    i = pl.program_id(0)
    chunk = (n_chunks - 1 - i) if reverse else i

    @pl.when(i == 0)
    def _():
        carry[...] = h0_ref[...]

    lam = lam_ref[...]
    softplus_neg_lam = jnp.maximum(-lam, 0.0) + jnp.log1p(jnp.exp(-jnp.abs(lam)))
    neg_half_c_softplus = (-0.5 * LRU_C) * softplus_neg_lam
    cw = cw_ref[...]
    cb = cb_ref[...]
    bg = bg_ref[...]
    row = lax.broadcasted_iota(jnp.int32, (t, w), 0)
    has_prev = chunk > 0
    has_next = chunk < n_chunks - 1
    for b in range(nb):
        cur = cur_ref[b].astype(F32)
        pv = prev_ref[b].astype(F32)
        nx = next_ref[b].astype(F32)
        pm2 = jnp.where(has_prev, pv[14:15], 0.0)
        pm1 = jnp.where(has_prev, pv[15:16], 0.0)
        nx0 = jnp.where(has_next, nx[0:1], 0.0)
        xm1 = jnp.where(row == 0, pm1, pltpu.roll(cur, 1, 0))
        xm2 = jnp.where(row == 0, pm2, jnp.where(row == 1, pm1, pltpu.roll(cur, 2, 0)))
        xp1 = jnp.where(row == t - 1, nx0, pltpu.roll(cur, t - 1, 0))
        x = cw[0:1] * xm2 + cw[1:2] * xm1 + cw[2:3] * cur + cw[3:4] * xp1 + cb
        tg = jnp.tanh(jnp.dot(x.astype(BF16), wg_ref[...], preferred_element_type=F32) + bg)
        log_a = neg_half_c_softplus * (1.0 + tg[:, :w])
        a = jnp.exp(log_a)
        z = 1.0 - a * a
        root = z * lax.rsqrt(jnp.maximum(z, TINY))
        bb = root * ((1.0 + tg[:, w:]) * (0.5 * x))
        for gi in range(n_lane_groups):
            a_scr[gi, pl.ds(b * pitch, t), :] = a[:, gi * LANES:(gi + 1) * LANES]
            b_scr[gi, pl.ds(b * pitch, t), :] = bb[:, gi * LANES:(gi + 1) * LANES]

    def step(s, hs):
        tt = (t - 1 - s) if reverse else s
        out = []
        for gi in range(n_lane_groups):
            a_t = a_scr[gi, pl.ds(tt, nb, stride=pitch), :]
            b_t = b_scr[gi, pl.ds(tt, nb, stride=pitch), :]
            h_new = a_t * hs[gi] + b_t
            b_scr[gi, pl.ds(tt, nb, stride=pitch), :] = h_new
            out.append(h_new)
        return tuple(out)

    h_init = tuple(carry[:, gi * LANES:(gi + 1) * LANES] for gi in range(n_lane_groups))
    h_fin = lax.fori_loop(0, t, step, h_init, unroll=8)
    for gi in range(n_lane_groups):
        carry[:, gi * LANES:(gi + 1) * LANES] = h_fin[gi]
        hend_ref[:, gi * LANES:(gi + 1) * LANES] = h_fin[gi]
    for b in range(nb):
        for gi in range(n_lane_groups):
            h_ref[b, :, gi * LANES:(gi + 1) * LANES] = b_scr[gi, pl.ds(b * pitch, t), :]


def _lru(p, col_block, cw, cb, wg, bg, lam, h0, reverse, t):
    nb, s, _ = p.shape
    w = LRU_W
    n_chunks = s // t
    halo = 16
    tb = t // halo
    last_halo = s // halo - 1
    if reverse:
        cidx = lambda i: n_chunks - 1 - i
    else:
        cidx = lambda i: i
    kern = functools.partial(_lru_kernel, reverse=reverse, n_chunks=n_chunks)
    return pl.pallas_call(
        kern,
        grid=(n_chunks,),
        in_specs=[
            pl.BlockSpec((nb, halo, w), lambda i: (0, jnp.maximum(cidx(i) * tb - 1, 0), col_block)),
            pl.BlockSpec((nb, t, w), lambda i: (0, cidx(i), col_block)),
            pl.BlockSpec((nb, halo, w), lambda i: (0, jnp.minimum((cidx(i) + 1) * tb, last_halo), col_block)),
            pl.BlockSpec((4, w), lambda i: (0, 0)),
            pl.BlockSpec((1, w), lambda i: (0, 0)),
            _resident((w, 2 * w), lambda i: (0, 0)),
            pl.BlockSpec((1, 2 * w), lambda i: (0, 0)),
            pl.BlockSpec((1, w), lambda i: (0, 0)),
            pl.BlockSpec((nb, w), lambda i: (0, 0)),
        ],
        out_specs=[
            pl.BlockSpec((nb, t, w), lambda i: (0, cidx(i), 0)),
            pl.BlockSpec((nb, w), lambda i: (0, 0)),
        ],
        out_shape=[
            jax.ShapeDtypeStruct((nb, s, w), F32),
            jax.ShapeDtypeStruct((nb, w), F32),
        ],
        scratch_shapes=[
            pltpu.VMEM((w // LANES, nb * (t + 8), LANES), F32),
            pltpu.VMEM((w // LANES, nb * (t + 8), LANES), F32),
            pltpu.VMEM((nb, w), F32),
        ],
        compiler_params=_cparams(("arbitrary",)),
        name="rglru_rev" if reverse else "rglru_fwd",
    )(p, p, p, cw, cb, wg, bg, lam, h0)


def _conv3_chunk(ref, n1, n_chunks, w, b):
    rows = FFT_R
    base = pl.multiple_of(n1 * rows, rows)
    cur = ref[pl.ds(base, rows), :].astype(F32)
    pbase = pl.multiple_of(jnp.maximum(base - CONV_HALO, 0), CONV_HALO)
    nbase = pl.multiple_of(jnp.minimum(base + rows, (n_chunks - 1) * rows), CONV_HALO)
    prev = jnp.where(n1 > 0, ref[pl.ds(pbase, CONV_HALO), :].astype(F32)[CONV_HALO - 1:CONV_HALO], 0.0)
    nxt = jnp.where(n1 < n_chunks - 1, ref[pl.ds(nbase, CONV_HALO), :].astype(F32)[0:1], 0.0)
    row = lax.broadcasted_iota(jnp.int32, cur.shape, 0)
    xm1 = jnp.where(row == 0, prev, pltpu.roll(cur, 1, 0))
    xp1 = jnp.where(row == rows - 1, nxt, pltpu.roll(cur, rows - 1, 0))
    return w[0:1] * xm1 + w[1:2] * cur + w[2:3] * xp1 + b


def _fwd_level1(u_scr, a_scr, f1):
    def body(grp, c):
        n2 = grp * FFT_NG
        xs = jnp.concatenate(
            [u_scr[pl.ds(n2 + j, FFT_H, stride=U_PITCH), :] for j in range(FFT_NG)], axis=1).astype(BF16)
        a = jnp.dot(f1, xs, preferred_element_type=F32)
        for j in range(FFT_NG):
            lanes = slice(j * LANES, (j + 1) * LANES)
            a_scr[pl.ds(n2 + j, FFT_K1P, stride=A_PITCH), :] = a[:FFT_K1P, lanes]
            a_scr[pl.ds(FFT_R + n2 + j, FFT_K1P, stride=A_PITCH), :] = a[FFT_K1P:, lanes]
        return c

    lax.fori_loop(0, FFT_R // FFT_NG, body, 0, unroll=4)


def _hyena_kernel(v_ref, x1_ref, x2_ref, cwv_ref, cw1_ref, cw2_ref, cbv_ref, cb1_ref, cb2_ref,
                  f1_ref, g_ref, f1i_ref, kf_ref, o_ref, u_scr, a_scr):
    n_chunks = FFT_H
    cwv, cbv = cwv_ref[...], cbv_ref[...]

    def load_v(n1, c):
        u_scr[pl.ds(pl.multiple_of(n1 * U_PITCH, 8), FFT_R), :] = _conv3_chunk(v_ref, n1, n_chunks, cwv, cbv)
        return c

    lax.fori_loop(0, n_chunks, load_v, 0)
    f1 = f1_ref[...]
    f1i = f1i_ref[...]
    tn = (((0,), (0,)), ((), ()))

    for order, (gate_ref, cw_ref, cb_ref) in enumerate(((x1_ref, cw1_ref, cb1_ref), (x2_ref, cw2_ref, cb2_ref))):
        _fwd_level1(u_scr, a_scr, f1)

        def mid(k1, c):
            base = pl.multiple_of(k1 * A_PITCH, 8)
            g = g_ref[k1]
            ak = a_scr[pl.ds(base, 2 * FFT_R), :].astype(BF16)
            xk = jnp.dot(g, ak, preferred_element_type=F32)
            kf = kf_ref[order, k1].astype(F32)
            xr, xi = xk[:FFT_R], xk[FFT_R:]
            kr, ki = kf[:FFT_R], kf[FFT_R:]
            y = jnp.concatenate([xr * kr - xi * ki, xr * ki + xi * kr], axis=0).astype(BF16)
            a_scr[pl.ds(base, 2 * FFT_R), :] = lax.dot_general(g, y, tn, preferred_element_type=F32)
            return c

        lax.fori_loop(0, FFT_K1, mid, 0, unroll=13)

        def inv2(grp, c):
            n2 = grp * FFT_NG
            bn = jnp.concatenate(
                [jnp.concatenate([a_scr[pl.ds(n2 + j, FFT_K1P, stride=A_PITCH), :],
                                  a_scr[pl.ds(FFT_R + n2 + j, FFT_K1P, stride=A_PITCH), :]], axis=0)
                 for j in range(FFT_NG)], axis=1).astype(BF16)
            y = jnp.dot(f1i, bn, preferred_element_type=F32)
            for j in range(FFT_NG):
                u_scr[pl.ds(n2 + j, FFT_H, stride=U_PITCH), :] = y[:, j * LANES:(j + 1) * LANES]
            return c

        lax.fori_loop(0, FFT_R // FFT_NG, inv2, 0, unroll=4)

        cw, cb = cw_ref[...], cb_ref[...]

        def gate(n1, c):
            ub = pl.multiple_of(n1 * U_PITCH, 8)
            z = _conv3_chunk(gate_ref, n1, n_chunks, cw, cb) * u_scr[pl.ds(ub, FFT_R), :]
            if order == HY_ORDER - 1:
                o_ref[pl.ds(pl.multiple_of(n1 * FFT_R, FFT_R), FFT_R), :] = z.astype(o_ref.dtype)
            else:
                u_scr[pl.ds(ub, FFT_R), :] = z
            return c

        lax.fori_loop(0, n_chunks, gate, 0)


def _dft_tables():
    n = FFT_R * FFT_R
    k1 = np.arange(FFT_K1)[:, None]
    n1 = np.arange(FFT_H)[None, :]
    th = 2.0 * np.pi * k1 * n1 / FFT_R
    f1 = np.zeros((2 * FFT_K1P, FFT_H), np.float64)
    f1[:FFT_K1] = np.cos(th)
    f1[FFT_K1P:FFT_K1P + FFT_K1] = -np.sin(th)
    wgt = np.full((FFT_K1, 1), 2.0)
    wgt[0] = 1.0
    wgt[-1] = 1.0
    f1i = np.zeros((FFT_H, 2 * FFT_K1P), np.float64)
    f1i[:, :FFT_K1] = (wgt * np.cos(th)).T / n
    f1i[:, FFT_K1P:FFT_K1P + FFT_K1] = (-wgt * np.sin(th)).T / n
    kk = np.arange(FFT_K1)[:, None, None] + FFT_R * np.arange(FFT_R)[None, :, None]
    n2 = np.arange(FFT_R)[None, None, :]
    ph = 2.0 * np.pi * ((kk * n2) % n) / n
    gr, gi = np.cos(ph), -np.sin(ph)
    g = np.concatenate([np.concatenate([gr, -gi], axis=2), np.concatenate([gi, gr], axis=2)], axis=1)
    return (jnp.asarray(f1, F32).astype(BF16), jnp.asarray(g, F32).astype(BF16),
            jnp.asarray(f1i, F32).astype(BF16))


def _hyena_latent(p, conv_w, conv_b, kf):
    b, s, _ = p.shape
    assert s == FFT_H * FFT_R
    cb_ = LANES
    ncb = HY_W // cb_
    f1, g, f1i = _dft_tables()
    conv_b2 = conv_b.reshape(1, -1)
    seq = lambda off: pl.BlockSpec((None, s, cb_), lambda c, i: (i, 0, off + c))
    cws = lambda off: pl.BlockSpec((3, cb_), lambda c, i: (0, off + c))
    cbs = lambda off: pl.BlockSpec((1, cb_), lambda c, i: (0, off + c))
    return pl.pallas_call(
        _hyena_kernel,
        grid=(ncb, b),
        in_specs=[
            seq(0), seq(ncb), seq(2 * ncb),
            cws(0), cws(ncb), cws(2 * ncb),
            cbs(0), cbs(ncb), cbs(2 * ncb),
            _resident(f1.shape, lambda c, i: (0, 0)),
            _resident(g.shape, lambda c, i: (0, 0, 0)),
            _resident(f1i.shape, lambda c, i: (0, 0)),
            _resident((HY_ORDER, FFT_K1, 2 * FFT_R, cb_), lambda c, i: (0, 0, 0, c)),
        ],
        out_specs=pl.BlockSpec((None, s, cb_), lambda c, i: (i, 0, c)),
        out_shape=jax.ShapeDtypeStruct((b, s, HY_W), BF16),
        scratch_shapes=[
            pltpu.VMEM((FFT_H * U_PITCH, cb_), F32),
            pltpu.VMEM((FFT_K1P * A_PITCH, cb_), F32),
        ],
        compiler_params=_cparams(("arbitrary", "arbitrary")),
        name="hyena_latent",
    )(p, p, p, conv_w, conv_w, conv_w, conv_b2, conv_b2, conv_b2, f1, g, f1i, kf)


def _hyena_ctx_kernel(u_ref, cw_ref, cb_ref, fc_ref, fi_ref, kf_ref, o_ref):
    l = u_ref.shape[0]
    half = fc_ref.shape[0] // 2
    u = u_ref[...].astype(F32)
    row = lax.broadcasted_iota(jnp.int32, u.shape, 0)
    cw = cw_ref[...]
    um1 = jnp.where(row == 0, 0.0, pltpu.roll(u, 1, 0))
    up1 = jnp.where(row == l - 1, 0.0, pltpu.roll(u, l - 1, 0))
    uc = cw[0:1] * um1 + cw[1:2] * u + cw[2:3] * up1 + cb_ref[...]
    z = uc[:, :HY_W]
    hi = lax.Precision.HIGHEST
    for order in range(HY_ORDER):
        gate = uc[:, (order + 1) * HY_W:(order + 2) * HY_W]
        xf = jnp.dot(fc_ref[...], z, precision=hi, preferred_element_type=F32)
        kf = kf_ref[order]
        xr, xi = xf[:half], xf[half:]
        kr, ki = kf[:half], kf[half:]
        y = jnp.concatenate([xr * kr - xi * ki, xr * ki + xi * kr], axis=0)
        z = gate * jnp.dot(fi_ref[...], y, precision=hi, preferred_element_type=F32)
    o_ref[...] = z.astype(o_ref.dtype)


def _ctx_dft_tables(l):
    n = 2 * l
    half = l + 8
    k = np.arange(l + 1)[:, None]
    t = np.arange(l)[None, :]
    th = 2.0 * np.pi * ((k * t) % n) / n
    fc = np.zeros((2 * half, l), np.float64)
    fc[:l + 1] = np.cos(th)
    fc[half:half + l + 1] = -np.sin(th)
    wgt = np.full((l + 1, 1), 2.0)
    wgt[0] = 1.0
    wgt[-1] = 1.0
    fi = np.zeros((l, 2 * half), np.float64)
    fi[:, :l + 1] = (wgt * np.cos(th)).T / n
    fi[:, half:half + l + 1] = (-wgt * np.sin(th)).T / n
    return jnp.asarray(fc, F32), jnp.asarray(fi, F32)


def _hyena_ctx(p, conv_w, conv_b, kf, fc, fi):
    b, l, _ = p.shape
    half = fc.shape[0] // 2
    wtot = 3 * HY_W
    return pl.pallas_call(
        _hyena_ctx_kernel,
        grid=(b,),
        in_specs=[
            pl.BlockSpec((None, l, wtot), lambda i: (i, 0, 0)),
            pl.BlockSpec((3, wtot), lambda i: (0, 0)),
            pl.BlockSpec((1, wtot), lambda i: (0, 0)),
            pl.BlockSpec((2 * half, l), lambda i: (0, 0)),
            pl.BlockSpec((l, 2 * half), lambda i: (0, 0)),
            pl.BlockSpec((HY_ORDER, 2 * half, HY_W), lambda i: (0, 0, 0)),
        ],
        out_specs=pl.BlockSpec((None, l, HY_W), lambda i: (i, 0, 0)),
        out_shape=jax.ShapeDtypeStruct((b, l, HY_W), BF16),
        compiler_params=_cparams(("parallel",)),
        name="hyena_ctx",
    )(p, conv_w, conv_b.reshape(1, -1), fc, fi, kf)


def _mix_ffn_kernel(*refs, recurrent):
    if recurrent:
        (x_ref, z_ref, hf_ref, hr_ref, gx_ref, wo_ref, g1_ref, gain_ref, sc_ref, sh_ref, g2_ref,
         w1_ref, w3_ref, w2_ref, o_ref, x1_scr, h_scr, t_scr) = refs
    else:
        (x_ref, m_ref, wo_ref, g1_ref, gain_ref, sc_ref, sh_ref, g2_ref,
         w1_ref, w3_ref, w2_ref, o_ref, x1_scr, h_scr, t_scr) = refs
    if recurrent:
        hw = z_ref.shape[-1]
        rec = ((hf_ref[...] + hr_ref[...]) * jax.nn.gelu(gx_ref[...].astype(F32))).astype(BF16)
        mixed = (jnp.dot(z_ref[...], wo_ref[:hw, :], preferred_element_type=F32)
                 + jnp.dot(rec, wo_ref[hw:, :], preferred_element_type=F32))
    else:
        mixed = jnp.dot(m_ref[...], wo_ref[...], preferred_element_type=F32)
    x1 = x_ref[...] + g1_ref[...] * mixed
    x1_scr[...] = x1
    h_scr[...] = _norm_mod(x1, gain_ref[...], sc_ref[...], sh_ref[...]).astype(BF16)

    d, ff = w1_ref.shape
    for c0 in range(0, ff, FFN_HIDDEN_CHUNK):
        cols = slice(c0, min(c0 + FFN_HIDDEN_CHUNK, ff))
        h = h_scr[...]
        a = jnp.dot(h, w1_ref[:, cols], preferred_element_type=F32)
        bgate = jnp.dot(h, w3_ref[:, cols], preferred_element_type=F32)
        t_scr[:, cols] = (jax.nn.silu(a) * bgate).astype(BF16)
    for n0 in range(0, d, FFN_OUT_CHUNK):
        cols = slice(n0, n0 + FFN_OUT_CHUNK)
        y = jnp.dot(t_scr[...], w2_ref[:, cols], preferred_element_type=F32)
        o_ref[:, cols] = x1_scr[:, cols] + g2_ref[:, cols] * y


def _mix_ffn(x, mixer_inputs, wo, g1, gain, sc, sh, g2, w1, w3, w2, tm, recurrent):
    b, s, d = x.shape
    ff = w1.shape[1]
    row = lambda width, col=0: pl.BlockSpec((None, tm, width), lambda i, r: (i, r, col))
    per_b = pl.BlockSpec((None, 1, d), lambda i, r: (i, 0, 0))
    if recurrent:
        z, hf, hr, p = mixer_inputs
        mix_specs = [row(HY_W), row(LRU_W), row(LRU_W), row(LRU_W, (3 * HY_W + LRU_W) // LRU_W)]
        mix_args = [z, hf, hr, p]
    else:
        (m,) = mixer_inputs
        mix_specs = [row(d)]
        mix_args = [m]
    return pl.pallas_call(
        functools.partial(_mix_ffn_kernel, recurrent=recurrent),
        grid=(b, s // tm),
        in_specs=[row(d)] + mix_specs + [
            _resident((d, d), lambda i, r: (0, 0)),
            per_b,
            pl.BlockSpec((1, d), lambda i, r: (0, 0)),
            per_b, per_b, per_b,
            _resident((d, ff), lambda i, r: (0, 0)),
            _resident((d, ff), lambda i, r: (0, 0)),
            _resident((ff, d), lambda i, r: (0, 0)),
        ],
        out_specs=row(d),
        out_shape=jax.ShapeDtypeStruct((b, s, d), F32),
        scratch_shapes=[
            pltpu.VMEM((tm, d), F32),
            pltpu.VMEM((tm, d), BF16),
            pltpu.VMEM((tm, ff), BF16),
        ],
        compiler_params=_cparams(("parallel", "parallel")),
        name="mix_ffn_rec" if recurrent else "mix_ffn_attn",
    )(x, *mix_args, wo, g1, gain, sc, sh, g2, w1, w3, w2)


def _filter_features(l):
    t = jnp.linspace(0.0, 1.0, l, dtype=F32)[:, None]
    bands = jnp.linspace(1e-4, HY_BANDS - 1, HY_BANDS, dtype=F32)
    w = 2.0 * math.pi * jnp.arange(l, dtype=F32)[:, None] / l
    z = jnp.concatenate([t, jnp.cos(bands * w), -jnp.sin(bands * w)], axis=-1)
    return jnp.pad(z, ((0, 0), (0, FILT_EMB_PAD - z.shape[1])))


def _filter_mlp_kernel(z_ref, w1_ref, b1_ref, w2_ref, b2_ref, fr_ref, o_ref):
    hp = lax.Precision.HIGHEST
    fr = fr_ref[...]
    h = jnp.sin(fr * (jnp.dot(z_ref[...], w1_ref[...], precision=hp, preferred_element_type=F32) + b1_ref[...]))
    o_ref[...] = jnp.sin(fr * (jnp.dot(h, w2_ref[...], precision=hp, preferred_element_type=F32) + b2_ref[...]))


def _filter_mlp(l, w1, b1, w2, b2, freq):
    z = _filter_features(l)
    w1p = jnp.pad(w1, ((0, FILT_EMB_PAD - w1.shape[0]), (0, 0)))
    hid = w2.shape[0]
    return pl.pallas_call(
        _filter_mlp_kernel,
        out_shape=jax.ShapeDtypeStruct((l, hid), F32),
        compiler_params=pltpu.CompilerParams(vmem_limit_bytes=VMEM_LIMIT_BYTES),
        name="filter_mlp",
    )(z, w1p, b1[None], w2, b2[None], freq[None])


def _filter_decay():
    max_decay = math.log(HY_DECAY_TARGET) / HY_FAST_DECAY
    min_decay = math.log(HY_DECAY_TARGET) / HY_SLOW_DECAY
    return jnp.abs(jnp.linspace(min_decay, max_decay, HY_W, dtype=F32))[None]


def _filter_taps(h, t, w3f, w3b, decay, row0):
    hp = lax.Precision.HIGHEST
    window = jnp.exp(-t * decay) + HY_MOD_SHIFT
    hf = jnp.dot(h, w3f, precision=hp, preferred_element_type=F32) * window
    hb = jnp.dot(h, w3b, precision=hp, preferred_element_type=F32) * window
    row = lax.broadcasted_iota(jnp.int32, hb.shape, 0) + row0
    return hf, jnp.where(row == 0, 0.0, hb)


def _filter_spec_latent_kernel(h_ref, t_ref, w3f_ref, w3b_ref, dec_ref, bias_ref, f1_ref, g_ref,
                               o_ref, uf_scr, ub_scr, af_scr, ab_scr):
    w3f, w3b, dec = w3f_ref[...], w3b_ref[...], dec_ref[...]

    def load(n1, acc):
        base = pl.multiple_of(n1 * FFT_R, FFT_R)
        hf, hb = _filter_taps(h_ref[pl.ds(base, FFT_R), :], t_ref[pl.ds(base, FFT_R), :], w3f, w3b, dec, base)
        ub = pl.multiple_of(n1 * U_PITCH, 8)
        uf_scr[pl.ds(ub, FFT_R), :] = hf
        ub_scr[pl.ds(ub, FFT_R), :] = hb
        return acc + jnp.sum(jnp.abs(hf) + jnp.abs(hb), axis=0, keepdims=True)

    l1 = lax.fori_loop(0, FFT_H, load, jnp.zeros((1, LANES), F32), unroll=4)
    inv = 1.0 / l1
    bias = bias_ref[...]
    f1 = f1_ref[...]
    _fwd_level1(uf_scr, af_scr, f1)
    _fwd_level1(ub_scr, ab_scr, f1)

    def level2(k1, c):
        base = pl.multiple_of(k1 * A_PITCH, 8)
        g = g_ref[k1]
        xf = jnp.dot(g, af_scr[pl.ds(base, 2 * FFT_R), :].astype(BF16), preferred_element_type=F32)
        xb = jnp.dot(g, ab_scr[pl.ds(base, 2 * FFT_R), :].astype(BF16), preferred_element_type=F32)
        kr = (xf[:FFT_R] + xb[:FFT_R]) * inv + bias
        ki = (xf[FFT_R:] - xb[FFT_R:]) * inv
        o_ref[k1] = jnp.concatenate([kr, ki], axis=0).astype(o_ref.dtype)
        return c

    lax.fori_loop(0, FFT_K1, level2, 0, unroll=5)


def _filter_spec_latent(h, w3, bias):
    l, hid = h.shape
    assert l == FFT_H * FFT_R
    f1, g, _ = _dft_tables()
    ncb = HY_W // LANES
    t = jnp.linspace(0.0, 1.0, l, dtype=F32)[:, None]
    return pl.pallas_call(
        _filter_spec_latent_kernel,
        grid=(HY_ORDER * ncb,),
        in_specs=[
            pl.BlockSpec((l, hid), lambda i: (0, 0)),
            pl.BlockSpec((l, 1), lambda i: (0, 0)),
            pl.BlockSpec((hid, LANES), lambda i: (0, i)),
            pl.BlockSpec((hid, LANES), lambda i: (0, HY_ORDER * ncb + i)),
            pl.BlockSpec((1, LANES), lambda i: (0, i % ncb)),
            pl.BlockSpec((None, 1, LANES), lambda i: (i // ncb, 0, i % ncb)),
            _resident(f1.shape, lambda i: (0, 0)),
            _resident(g.shape, lambda i: (0, 0, 0)),
        ],
        out_specs=pl.BlockSpec((None, FFT_K1, 2 * FFT_R, LANES), lambda i: (i // ncb, 0, 0, i % ncb)),
        out_shape=jax.ShapeDtypeStruct((HY_ORDER, FFT_K1, 2 * FFT_R, HY_W), BF16),
        scratch_shapes=[
            pltpu.VMEM((FFT_H * U_PITCH, LANES), F32),
            pltpu.VMEM((FFT_H * U_PITCH, LANES), F32),
            pltpu.VMEM((FFT_K1P * A_PITCH, LANES), F32),
            pltpu.VMEM((FFT_K1P * A_PITCH, LANES), F32),
        ],
        compiler_params=_cparams(("arbitrary",)),
        name="filter_spec_latent",
    )(h, t, w3, w3, _filter_decay(), bias[:, None, :], f1, g)


def _filter_spec_ctx_kernel(h_ref, t_ref, w3f_ref, w3b_ref, dec_ref, bias_ref, fc_ref, o_ref):
    hp = lax.Precision.HIGHEST
    half = fc_ref.shape[0] // 2
    hf, hb = _filter_taps(h_ref[...], t_ref[...], w3f_ref[...], w3b_ref[...], dec_ref[...], 0)
    inv = 1.0 / jnp.sum(jnp.abs(hf) + jnp.abs(hb), axis=0, keepdims=True)
    xf = jnp.dot(fc_ref[...], hf, precision=hp, preferred_element_type=F32)
    xb = jnp.dot(fc_ref[...], hb, precision=hp, preferred_element_type=F32)
    kr = (xf[:half] + xb[:half]) * inv + bias_ref[...]
    ki = (xf[half:] - xb[half:]) * inv
    o_ref[...] = jnp.concatenate([kr, ki], axis=0)


def _filter_spec_ctx(h, w3, bias, fc):
    l, hid = h.shape
    ncb = HY_W // LANES
    t = jnp.linspace(0.0, 1.0, l, dtype=F32)[:, None]
    return pl.pallas_call(
        _filter_spec_ctx_kernel,
        grid=(HY_ORDER * ncb,),
        in_specs=[
            pl.BlockSpec((l, hid), lambda i: (0, 0)),
            pl.BlockSpec((l, 1), lambda i: (0, 0)),
            pl.BlockSpec((hid, LANES), lambda i: (0, i)),
            pl.BlockSpec((hid, LANES), lambda i: (0, HY_ORDER * ncb + i)),
            pl.BlockSpec((1, LANES), lambda i: (0, i % ncb)),
            pl.BlockSpec((None, 1, LANES), lambda i: (i // ncb, 0, i % ncb)),
            pl.BlockSpec(fc.shape, lambda i: (0, 0)),
        ],
        out_specs=pl.BlockSpec((None, fc.shape[0], LANES), lambda i: (i // ncb, 0, i % ncb)),
        out_shape=jax.ShapeDtypeStruct((HY_ORDER, fc.shape[0], HY_W), F32),
        compiler_params=_cparams(("parallel",)),
        name="filter_spec_ctx",
    )(h, t, w3, w3, _filter_decay(), bias[:, None, :], fc)


def _mod_kernel(c_ref, w_ref, b_ref, o_ref):
    act = jax.nn.silu(c_ref[...])
    o_ref[...] = jnp.dot(act, w_ref[...], precision=lax.Precision.HIGHEST,
                         preferred_element_type=F32) + b_ref[...]


def _modulations(c, c_ctx, w_mod, b_mod):
    depth, d, n = w_mod.shape
    nb = c.shape[0]
    rows = 2 * nb
    cc = jnp.zeros((rows, d), F32).at[:nb].set(c).at[nb].set(c_ctx)
    tn = n // 4
    return pl.pallas_call(
        _mod_kernel,
        grid=(depth, n // tn),
        in_specs=[
            pl.BlockSpec((rows, d), lambda i, j: (0, 0)),
            pl.BlockSpec((None, d, tn), lambda i, j: (i, 0, j)),
            pl.BlockSpec((None, 1, tn), lambda i, j: (i, 0, j)),
        ],
        out_specs=pl.BlockSpec((None, rows, tn), lambda i, j: (i, 0, j)),
        out_shape=jax.ShapeDtypeStruct((depth, rows, n), F32),
        compiler_params=_cparams(("parallel", "parallel")),
        name="adaln_mod",
    )(cc, w_mod, b_mod[:, None, :])


def _block_diag(w):
    h, bw, _ = w.shape
    eye = jnp.eye(h, dtype=w.dtype)
    return (eye[:, None, :, None] * w[:, :, None, :]).reshape(h * bw, h * bw)


def _dup_kv_columns(w):
    nq = N_HEADS * HEAD_DIM
    d = w.shape[0]
    kv = w[:, nq:].reshape(d, 2 * N_KV, 1, HEAD_DIM)
    kv = jnp.broadcast_to(kv, (d, 2 * N_KV, LANES // HEAD_DIM, HEAD_DIM)).reshape(d, -1)
    return jnp.concatenate([w[:, :nq], kv], axis=1)


def _split_mod(m):
    return [t[:, None, :] for t in jnp.split(m, 6, axis=-1)]


def _rope_tables(s):
    half = HEAD_DIM // 2
    nf = half // 2
    inv = jnp.power(ROPE_BASE, -jnp.arange(nf, dtype=F32) / nf)
    pos = jnp.arange(s, dtype=jnp.int32)
    row = (pos // GRID_W).astype(F32)[:, None] * inv
    col = (pos % GRID_W).astype(F32)[:, None] * inv
    cos = jnp.concatenate([jnp.cos(row), jnp.cos(row), jnp.cos(col), jnp.cos(col)], axis=-1)
    sin = jnp.concatenate([-jnp.sin(row), jnp.sin(row), -jnp.sin(col), jnp.sin(col)], axis=-1)
    return jnp.tile(cos, (1, LANES // HEAD_DIM)), jnp.tile(sin, (1, LANES // HEAD_DIM))


def _rope_gain_tables(cos, sin, gain, scale):
    g = jnp.tile(gain, LANES // HEAD_DIM)
    first_half = (jnp.arange(LANES) % 32) < 16
    g_partner = jnp.where(first_half, jnp.roll(g, -16), jnp.roll(g, 16))
    return cos * (g * scale)[None], sin * (g_partner * scale)[None]


def kernel(x, c, ctx, c_ctx, norm1, norm2, w_mod, b_mod, ffn_w1, ffn_w3, ffn_w2, ab_w_in, hy_conv_w, hy_conv_b, hy_f_w1, hy_f_b1, hy_f_w2, hy_f_b2, hy_f_w3, hy_f_freq, hy_bias, lru_conv_w, lru_conv_b, lru_w_a, lru_b_a, lru_w_i, lru_b_i, lru_lam, ab_w_out, at_w_qkv, at_q_gain, at_k_gain, at_sink, at_w_o):
    nb, s, d = x.shape
    cl = ctx.shape[1]
    bf = lambda a: a.astype(BF16)

    mods = _modulations(c, c_ctx, w_mod, b_mod)
    sh1x, sc1x, g1x, sh2x, sc2x, g2x = _split_mod(mods[0, :nb])
    sh1c, sc1c, g1c, sh2c, sc2c, g2c = _split_mod(jnp.broadcast_to(mods[0, nb:nb + 1], (nb, mods.shape[-1])))
    gain1 = norm1[0][None]
    gain2 = norm2[0][None]
    w_in = bf(ab_w_in[0])
    px = _proj(x, gain1, sc1x, sh1x, w_in, 512)
    pc = _proj(ctx, gain1, sc1c, sh1c, w_in, cl)

    mlp = (hy_f_w1[0], hy_f_b1[0], hy_f_w2[0], hy_f_b2[0], hy_f_freq[0])
    fc, fi = _ctx_dft_tables(cl)
    kf_x = _filter_spec_latent(_filter_mlp(s, *mlp), hy_f_w3[0], hy_bias[0])
    kf_c = _filter_spec_ctx(_filter_mlp(cl, *mlp), hy_f_w3[0], hy_bias[0], fc)
    zx = _hyena_latent(px, hy_conv_w[0], hy_conv_b[0], kf_x)
    zc = _hyena_ctx(pc, hy_conv_w[0], hy_conv_b[0], kf_c, fc, fi)

    lru_col = 3 * HY_W // LRU_W
    lcb = lru_conv_b[0][None]
    h0 = jnp.zeros((nb, LRU_W), F32)
    hx, hc = [], []
    for dr, reverse in enumerate((False, True)):
        wg = bf(0.5 * jnp.concatenate([_block_diag(lru_w_a[0, dr]), _block_diag(lru_w_i[0, dr])], axis=1))
        bg = 0.5 * jnp.concatenate([lru_b_a[0, dr], lru_b_i[0, dr]])[None]
        lam = lru_lam[0, dr][None]
        hcs, h_end = _lru(pc, lru_col, lru_conv_w[0], lcb, wg, bg, lam, h0, reverse, cl)
        hxs, _ = _lru(px, lru_col, lru_conv_w[0], lcb, wg, bg, lam, h_end, reverse, 256)
        hx.append(hxs)
        hc.append(hcs)

    w_out = bf(ab_w_out[0])
    w1, w3, w2 = bf(ffn_w1[0]), bf(ffn_w3[0]), bf(ffn_w2[0])
    x = _mix_ffn(x, (zx, hx[0], hx[1], px), w_out, g1x, gain2, sc2x, sh2x, g2x, w1, w3, w2, 512, True)
    ctx = _mix_ffn(ctx, (zc, hc[0], hc[1], pc), w_out, g1c, gain2, sc2c, sh2c, g2c, w1, w3, w2, cl, True)

    sh1x, sc1x, g1x, sh2x, sc2x, g2x = _split_mod(mods[1, :nb])
    sh1c, sc1c = _split_mod(jnp.broadcast_to(mods[1, nb:nb + 1], (nb, mods.shape[-1])))[:2]
    gain1 = norm1[1][None]
    gain2 = norm2[1][None]
    w_qkv = bf(_dup_kv_columns(at_w_qkv[0]))
    cos, sin = _rope_tables(s)
    q_scale = HEAD_DIM ** -0.5
    q, k, v = _qkv(x, gain1, sc1x, sh1x, w_qkv,
                   *_rope_gain_tables(cos, sin, at_q_gain[0], q_scale),
                   *_rope_gain_tables(cos, sin, at_k_gain[0], 1.0), 512)
    no_rot = (jnp.ones((cl, LANES), F32), jnp.zeros((cl, LANES), F32))
    _, kx, vx = _qkv(ctx, gain1, sc1c, sh1c, w_qkv,
                     *_rope_gain_tables(*no_rot, at_q_gain[0], q_scale),
                     *_rope_gain_tables(*no_rot, at_k_gain[0], 1.0), cl)
    o = _attention(q, k, v, kx, vx, at_sink[0], WINDOW)
    w1, w3, w2 = bf(ffn_w1[1]), bf(ffn_w3[1]), bf(ffn_w2[1])
    return _mix_ffn(x, (o,), bf(at_w_o[0]), g1x, gain2, sc2x, sh2x, g2x, w1, w3, w2, 512, False)
```

```python
import functools
import math

import numpy as np
import jax
import jax.numpy as jnp
from jax import lax
from jax.experimental import pallas as pl
from jax.experimental.pallas import tpu as pltpu

F32 = jnp.float32
BF16 = jnp.bfloat16

EPS = 1e-6
D_MODEL = 1024
GRID_W = 64
HY_W = 512
HY_ORDER = 2
HY_BANDS = 16
HY_MOD_SHIFT = 0.05
HY_FAST_DECAY = 0.3
HY_SLOW_DECAY = 1.5
HY_DECAY_TARGET = 1e-2
LRU_W = 512
LRU_HEADS = 8
LRU_C = 8.0
HEAD_DIM = 64
N_HEADS = 16
N_KV = 4
GROUP = N_HEADS // N_KV
WINDOW = 128
ROPE_BASE = 10000.0
NEG_INF = -1e30
TINY = 1e-30

VMEM_LIMIT_BYTES = 58 * 1024 * 1024
LANES = 128

FFT_R = 128
FFT_H = FFT_R // 2
FFT_K1 = FFT_R // 2 + 1
FFT_K1P = 72
FFT_NG = 4
CONV_HALO = 16
ROW_TILE = 512
LRU_TIME_CHUNK = 256
ATTN_QBLOCKS = 2
FFN_HIDDEN_CHUNK = 256
FFN_OUT_CHUNK = 256
FILT_EMB_PAD = 40
U_PITCH = FFT_R + 8
A_PITCH = 2 * FFT_R + 8


def _cparams(sem):
    return pltpu.CompilerParams(dimension_semantics=sem, vmem_limit_bytes=VMEM_LIMIT_BYTES)


def _resident(block_shape, index_map):
    return pl.BlockSpec(block_shape, index_map, pipeline_mode=pl.Buffered(1))


def _norm_mod(x, gain, scale, shift):
    y = x * lax.rsqrt(jnp.mean(x * x, axis=-1, keepdims=True) + EPS)
    return (y * gain) * (1.0 + scale) + shift


def _proj_kernel(x_ref, gain_ref, sc_ref, sh_ref, w_ref, o_ref, *, n_chunk):
    h = _norm_mod(x_ref[...], gain_ref[...], sc_ref[...], sh_ref[...]).astype(BF16)
    n_out = o_ref.shape[-1]
    for n0 in range(0, n_out, n_chunk):
        o_ref[:, n0:n0 + n_chunk] = jnp.dot(
            h, w_ref[:, n0:n0 + n_chunk], preferred_element_type=F32).astype(o_ref.dtype)


def _proj(x, gain, sc, sh, w, tm):
    b, s, d = x.shape
    n = w.shape[1]
    return pl.pallas_call(
        functools.partial(_proj_kernel, n_chunk=512),
        grid=(b, s // tm),
        in_specs=[
            pl.BlockSpec((None, tm, d), lambda i, j: (i, j, 0)),
            pl.BlockSpec((1, d), lambda i, j: (0, 0)),
            pl.BlockSpec((None, 1, d), lambda i, j: (i, 0, 0)),
            pl.BlockSpec((None, 1, d), lambda i, j: (i, 0, 0)),
            _resident((d, n), lambda i, j: (0, 0)),
        ],
        out_specs=pl.BlockSpec((None, tm, n), lambda i, j: (i, j, 0)),
        out_shape=jax.ShapeDtypeStruct((b, s, n), BF16),
        compiler_params=_cparams(("parallel", "parallel")),
        name="proj_in",
    )(x, gain, sc, sh, w)


def _qkv_kernel(x_ref, gain_ref, sc_ref, sh_ref, w_ref, hm_ref, cq_ref, sq_ref, ck_ref, sk_ref,
                q_ref, k_ref, v_ref):
    tm = x_ref.shape[0]
    nq = q_ref.shape[-1]
    nkv = k_ref.shape[-1]
    wide = 2 * LANES
    h = _norm_mod(x_ref[...], gain_ref[...], sc_ref[...], sh_ref[...]).astype(BF16)
    hm = hm_ref[...]
    lane = lax.broadcasted_iota(jnp.int32, (tm, wide), 1)
    first_half = (lane % 32) < 16

    def norm_rope(y, cos_g, sin_g):
        inv = lax.rsqrt(jnp.dot((y * y).astype(BF16), hm, preferred_element_type=F32) + EPS)
        partner = jnp.where(first_half, pltpu.roll(y, wide - 16, 1), pltpu.roll(y, 16, 1))
        return (y * cos_g + partner * sin_g) * inv

    def tables(c_ref, s_ref):
        c, s = c_ref[...], s_ref[...]
        return jnp.concatenate([c, c], axis=1), jnp.concatenate([s, s], axis=1)

    cq, sq = tables(cq_ref, sq_ref)
    for n0 in range(0, nq, 512):
        y = jnp.dot(h, w_ref[:, n0:n0 + 512], preferred_element_type=F32)
        for j in range(512 // wide):
            q_ref[:, n0 + j * wide:n0 + (j + 1) * wide] = norm_rope(
                y[:, j * wide:(j + 1) * wide], cq, sq).astype(BF16)
    ck, sk = tables(ck_ref, sk_ref)
    y = jnp.dot(h, w_ref[:, nq:nq + nkv], preferred_element_type=F32)
    for j in range(nkv // wide):
        k_ref[:, j * wide:(j + 1) * wide] = norm_rope(y[:, j * wide:(j + 1) * wide], ck, sk).astype(BF16)
    v_ref[...] = jnp.dot(h, w_ref[:, nq + nkv:], preferred_element_type=F32).astype(BF16)


def _head_mean_matrix():
    head = np.arange(2 * LANES) // HEAD_DIM
    return jnp.asarray((head[:, None] == head[None, :]) / HEAD_DIM, F32).astype(BF16)


def _qkv(x, gain, sc, sh, w, cq, sq, ck, sk, tm):
    b, s, d = x.shape
    nq = N_HEADS * HEAD_DIM
    nkv = (w.shape[1] - nq) // 2
    tab = pl.BlockSpec((tm, LANES), lambda i, j: (j, 0))
    return pl.pallas_call(
        _qkv_kernel,
        grid=(b, s // tm),
        in_specs=[
            pl.BlockSpec((None, tm, d), lambda i, j: (i, j, 0)),
            pl.BlockSpec((1, d), lambda i, j: (0, 0)),
            pl.BlockSpec((None, 1, d), lambda i, j: (i, 0, 0)),
            pl.BlockSpec((None, 1, d), lambda i, j: (i, 0, 0)),
            _resident((d, nq + 2 * nkv), lambda i, j: (0, 0)),
            pl.BlockSpec((2 * LANES, 2 * LANES), lambda i, j: (0, 0)),
            tab, tab, tab, tab,
        ],
        out_specs=[
            pl.BlockSpec((None, tm, nq), lambda i, j: (i, j, 0)),
            pl.BlockSpec((None, tm, nkv), lambda i, j: (i, j, 0)),
            pl.BlockSpec((None, tm, nkv), lambda i, j: (i, j, 0)),
        ],
        out_shape=[
            jax.ShapeDtypeStruct((b, s, nq), BF16),
            jax.ShapeDtypeStruct((b, s, nkv), BF16),
            jax.ShapeDtypeStruct((b, s, nkv), BF16),
        ],
        compiler_params=_cparams(("parallel", "parallel")),
        name="qkv",
    )(x, gain, sc, sh, w, _head_mean_matrix(), cq, sq, ck, sk)


def _attn_kernel(sink_ref, bias_a_ref, bias_b_ref, q_ref, k0_ref, k1_ref, k2_ref, k3_ref,
                 v0_ref, v1_ref, v2_ref, v3_ref, kx_ref, vx_ref, o_ref):
    blk = k0_ref.shape[0]
    n_loc = 3 * blk
    n_keys = n_loc + kx_ref.shape[0]
    k_refs = (k0_ref, k1_ref, k2_ref, k3_ref)
    v_refs = (v0_ref, v1_ref, v2_ref, v3_ref)
    lane = lax.broadcasted_iota(jnp.int32, (blk, LANES), 1)
    low = lane < HEAD_DIM
    keep_low = jnp.where(low, 1.0, 0.0).astype(BF16)
    keep_high = jnp.where(low, 0.0, 1.0).astype(BF16)
    ones = jnp.ones((n_keys, LANES), BF16)
    nt = (((1,), (1,)), ((), ()))
    for qb, bias_ref in enumerate((bias_a_ref, bias_b_ref)):
        rows = slice(qb * blk, (qb + 1) * blk)
        bias = bias_ref[...]
        for g in range(N_KV):
            gl = slice(g * LANES, (g + 1) * LANES)
            k_all = jnp.concatenate([r[:, gl] for r in k_refs[qb:qb + 3]] + [kx_ref[:, gl]], axis=0)
            v_all = jnp.concatenate([r[:, gl] for r in v_refs[qb:qb + 3]] + [vx_ref[:, gl]], axis=0)
            v_aug = jnp.concatenate([v_all, ones], axis=1)
            q4 = jnp.concatenate(
                [q_ref[rows, ((GROUP * g + j) // 2) * LANES:((GROUP * g + j) // 2 + 1) * LANES]
                 * (keep_low if j % 2 == 0 else keep_high) for j in range(GROUP)], axis=0)
            s = lax.dot_general(q4, k_all, nt, preferred_element_type=F32)
            s_loc = jnp.concatenate([s[:, :blk] + bias[:, :blk], s[:, blk:2 * blk],
                                     s[:, 2 * blk:n_loc] + bias[:, 2 * blk:]], axis=1)
            s_ctx = s[:, n_loc:]
            s_sink = jnp.concatenate(
                [jnp.full((blk, 1), sink_ref[GROUP * g + j], F32) for j in range(GROUP)], axis=0)
            m = jnp.maximum(jnp.maximum(jnp.max(s_loc, axis=-1, keepdims=True),
                                        jnp.max(s_ctx, axis=-1, keepdims=True)), s_sink)
            p = jnp.exp(jnp.concatenate([s_loc - m, s_ctx - m], axis=1).astype(BF16))
            pv = jnp.dot(p, v_aug, preferred_element_type=F32)
            o4 = pv[:, :LANES] / (pv[:, LANES:] + jnp.exp(s_sink - m))
            for pair in range(GROUP // 2):
                slab = jnp.where(low, o4[2 * pair * blk:(2 * pair + 1) * blk],
                                 o4[(2 * pair + 1) * blk:(2 * pair + 2) * blk])
                col = (GROUP // 2) * g + pair
                o_ref[rows, col * LANES:(col + 1) * LANES] = slab.astype(BF16)


def _attention(q, k, v, kx, vx, sink, blk):
    b, s, nq = q.shape
    nkv = k.shape[-1]
    cx = kx.shape[1]
    nb = s // blk
    steps = nb // ATTN_QBLOCKS
    qspec = pl.BlockSpec((None, ATTN_QBLOCKS * blk, nq), lambda i, j, *_: (i, j, 0))

    def kv_spec(off):
        return pl.BlockSpec(
            (None, blk, nkv), lambda i, j, *_: (i, jnp.clip(ATTN_QBLOCKS * j + off, 0, nb - 1), 0))

    kv_specs = [kv_spec(off) for off in range(-1, ATTN_QBLOCKS + 1)]
    cxs = pl.BlockSpec((None, cx, nkv), lambda i, j, *_: (i, 0, 0))
    r = np.arange(GROUP * blk)[:, None] % blk
    c = np.arange(3 * blk)[None, :]
    band = np.abs(c - blk - r) <= WINDOW
    variants = [band & (c >= blk), band, band & (c < 2 * blk)]
    bias = jnp.asarray(np.where(np.stack(variants), 0.0, NEG_INF), F32)
    bias_shape = (None, GROUP * blk, 3 * blk)
    bias_first = pl.BlockSpec(bias_shape, lambda i, j, *_: (jnp.where(j == 0, 0, 1), 0, 0))
    bias_last = pl.BlockSpec(bias_shape, lambda i, j, *_: (jnp.where(j == steps - 1, 2, 1), 0, 0))
    return pl.pallas_call(
        _attn_kernel,
        grid_spec=pltpu.PrefetchScalarGridSpec(
            num_scalar_prefetch=1,
            grid=(b, steps),
            in_specs=[bias_first, bias_last, qspec] + kv_specs + kv_specs + [cxs, cxs],
            out_specs=pl.BlockSpec((None, ATTN_QBLOCKS * blk, nq), lambda i, j, *_: (i, j, 0)),
        ),
        out_shape=jax.ShapeDtypeStruct((b, s, nq), BF16),
        compiler_params=_cparams(("parallel", "parallel")),
        name="window_attn",
    )(sink, bias, bias, q, k, k, k, k, v, v, v, v, kx, vx)


def _lru_kernel(prev_ref, cur_ref, next_ref, cw_ref, cb_ref, wg_ref, bg_ref, lam_ref, h0_ref,
                h_ref, hend_ref, a_scr, b_scr, carry, *, reverse, n_chunks):
    nb, t, w = cur_ref.shape
    pitch = t + 8
    n_lane_groups = w // LANES
    i = pl.program_id(0)
    chunk = (n_chunks - 1 - i) if reverse else i

    @pl.when(i == 0)
    def _():
        carry[...] = h0_ref[...]

    lam = lam_ref[...]
    softplus_neg_lam = jnp.maximum(-lam, 0.0) + jnp.log1p(jnp.exp(-jnp.abs(lam)))
    neg_half_c_softplus = (-0.5 * LRU_C) * softplus_neg_lam
    cw = cw_ref[...]
    cb = cb_ref[...]
    bg = bg_ref[...]
    row = lax.broadcasted_iota(jnp.int32, (t, w), 0)
    has_prev = chunk > 0
    has_next = chunk < n_chunks - 1
    for b in range(nb):
        cur = cur_ref[b].astype(F32)
        pv = prev_ref[b].astype(F32)
        nx = next_ref[b].astype(F32)
        pm2 = jnp.where(has_prev, pv[14:15], 0.0)
        pm1 = jnp.where(has_prev, pv[15:16], 0.0)
        nx0 = jnp.where(has_next, nx[0:1], 0.0)
        xm1 = jnp.where(row == 0, pm1, pltpu.roll(cur, 1, 0))
        xm2 = jnp.where(row == 0, pm2, jnp.where(row == 1, pm1, pltpu.roll(cur, 2, 0)))
        xp1 = jnp.where(row == t - 1, nx0, pltpu.roll(cur, t - 1, 0))
        x = cw[0:1] * xm2 + cw[1:2] * xm1 + cw[2:3] * cur + cw[3:4] * xp1 + cb
        tg = jnp.tanh(jnp.dot(x.astype(BF16), wg_ref[...], preferred_element_type=F32) + bg)
        log_a = neg_half_c_softplus * (1.0 + tg[:, :w])
        a = jnp.exp(log_a)
        z = 1.0 - a * a
        root = z * lax.rsqrt(jnp.maximum(z, TINY))
        bb = root * ((1.0 + tg[:, w:]) * (0.5 * x))
        for gi in range(n_lane_groups):
            a_scr[gi, pl.ds(b * pitch, t), :] = a[:, gi * LANES:(gi + 1) * LANES]
            b_scr[gi, pl.ds(b * pitch, t), :] = bb[:, gi * LANES:(gi + 1) * LANES]

    def step(s, hs):
        tt = (t - 1 - s) if reverse else s
        out = []
        for gi in range(n_lane_groups):
            a_t = a_scr[gi, pl.ds(tt, nb, stride=pitch), :]
            b_t = b_scr[gi, pl.ds(tt, nb, stride=pitch), :]
            h_new = a_t * hs[gi] + b_t
            b_scr[gi, pl.ds(tt, nb, stride=pitch), :] = h_new
            out.append(h_new)
        return tuple(out)

    h_init = tuple(carry[:, gi * LANES:(gi + 1) * LANES] for gi in range(n_lane_groups))
    h_fin = lax.fori_loop(0, t, step, h_init, unroll=8)
    for gi in range(n_lane_groups):
        carry[:, gi * LANES:(gi + 1) * LANES] = h_fin[gi]
        hend_ref[:, gi * LANES:(gi + 1) * LANES] = h_fin[gi]
    for b in range(nb):
        for gi in range(n_lane_groups):
            h_ref[b, :, gi * LANES:(gi + 1) * LANES] = b_scr[gi, pl.ds(b * pitch, t), :]


def _lru(p, col_block, cw, cb, wg, bg, lam, h0, reverse, t):
    nb, s, _ = p.shape
    w = LRU_W
    n_chunks = s // t
    halo = 16
    tb = t // halo
    last_halo = s // halo - 1
    if reverse:
        cidx = lambda i: n_chunks - 1 - i
    else:
        cidx = lambda i: i
    kern = functools.partial(_lru_kernel, reverse=reverse, n_chunks=n_chunks)
    return pl.pallas_call(
        kern,
        grid=(n_chunks,),
        in_specs=[
            pl.BlockSpec((nb, halo, w), lambda i: (0, jnp.maximum(cidx(i) * tb - 1, 0), col_block)),
            pl.BlockSpec((nb, t, w), lambda i: (0, cidx(i), col_block)),
            pl.BlockSpec((nb, halo, w), lambda i: (0, jnp.minimum((cidx(i) + 1) * tb, last_halo), col_block)),
            pl.BlockSpec((4, w), lambda i: (0, 0)),
            pl.BlockSpec((1, w), lambda i: (0, 0)),
            _resident((w, 2 * w), lambda i: (0, 0)),
            pl.BlockSpec((1, 2 * w), lambda i: (0, 0)),
            pl.BlockSpec((1, w), lambda i: (0, 0)),
            pl.BlockSpec((nb, w), lambda i: (0, 0)),
        ],
        out_specs=[
            pl.BlockSpec((nb, t, w), lambda i: (0, cidx(i), 0)),
            pl.BlockSpec((nb, w), lambda i: (0, 0)),
        ],
        out_shape=[
            jax.ShapeDtypeStruct((nb, s, w), F32),
            jax.ShapeDtypeStruct((nb, w), F32),
        ],
        scratch_shapes=[
            pltpu.VMEM((w // LANES, nb * (t + 8), LANES), F32),
            pltpu.VMEM((w // LANES, nb * (t + 8), LANES), F32),
            pltpu.VMEM((nb, w), F32),
        ],
        compiler_params=_cparams(("arbitrary",)),
        name="rglru_rev" if reverse else "rglru_fwd",
    )(p, p, p, cw, cb, wg, bg, lam, h0)


def _conv3_chunk(ref, n1, n_chunks, w, b):
    rows = FFT_R
    base = pl.multiple_of(n1 * rows, rows)
    cur = ref[pl.ds(base, rows), :].astype(F32)
    pbase = pl.multiple_of(jnp.maximum(base - CONV_HALO, 0), CONV_HALO)
    nbase = pl.multiple_of(jnp.minimum(base + rows, (n_chunks - 1) * rows), CONV_HALO)
    prev = jnp.where(n1 > 0, ref[pl.ds(pbase, CONV_HALO), :].astype(F32)[CONV_HALO - 1:CONV_HALO], 0.0)
    nxt = jnp.where(n1 < n_chunks - 1, ref[pl.ds(nbase, CONV_HALO), :].astype(F32)[0:1], 0.0)
    row = lax.broadcasted_iota(jnp.int32, cur.shape, 0)
    xm1 = jnp.where(row == 0, prev, pltpu.roll(cur, 1, 0))
    xp1 = jnp.where(row == rows - 1, nxt, pltpu.roll(cur, rows - 1, 0))
    return w[0:1] * xm1 + w[1:2] * cur + w[2:3] * xp1 + b


def _fwd_level1(u_scr, a_scr, f1):
    def body(grp, c):
        n2 = grp * FFT_NG
        xs = jnp.concatenate(
            [u_scr[pl.ds(n2 + j, FFT_H, stride=U_PITCH), :] for j in range(FFT_NG)], axis=1).astype(BF16)
        a = jnp.dot(f1, xs, preferred_element_type=F32)
        for j in range(FFT_NG):
            lanes = slice(j * LANES, (j + 1) * LANES)
            a_scr[pl.ds(n2 + j, FFT_K1P, stride=A_PITCH), :] = a[:FFT_K1P, lanes]
            a_scr[pl.ds(FFT_R + n2 + j, FFT_K1P, stride=A_PITCH), :] = a[FFT_K1P:, lanes]
        return c

    lax.fori_loop(0, FFT_R // FFT_NG, body, 0, unroll=4)


def _hyena_kernel(v_ref, x1_ref, x2_ref, cwv_ref, cw1_ref, cw2_ref, cbv_ref, cb1_ref, cb2_ref,
                  f1_ref, g_ref, f1i_ref, kf_ref, o_ref, u_scr, a_scr):
    n_chunks = FFT_H
    cwv, cbv = cwv_ref[...], cbv_ref[...]

    def load_v(n1, c):
        u_scr[pl.ds(pl.multiple_of(n1 * U_PITCH, 8), FFT_R), :] = _conv3_chunk(v_ref, n1, n_chunks, cwv, cbv)
        return c

    lax.fori_loop(0, n_chunks, load_v, 0)
    f1 = f1_ref[...]
    f1i = f1i_ref[...]
    tn = (((0,), (0,)), ((), ()))

    for order, (gate_ref, cw_ref, cb_ref) in enumerate(((x1_ref, cw1_ref, cb1_ref), (x2_ref, cw2_ref, cb2_ref))):
        _fwd_level1(u_scr, a_scr, f1)

        def mid(k1, c):
            base = pl.multiple_of(k1 * A_PITCH, 8)
            g = g_ref[k1]
            ak = a_scr[pl.ds(base, 2 * FFT_R), :].astype(BF16)
            xk = jnp.dot(g, ak, preferred_element_type=F32)
            kf = kf_ref[order, k1].astype(F32)
            xr, xi = xk[:FFT_R], xk[FFT_R:]
            kr, ki = kf[:FFT_R], kf[FFT_R:]
            y = jnp.concatenate([xr * kr - xi * ki, xr * ki + xi * kr], axis=0).astype(BF16)
            a_scr[pl.ds(base, 2 * FFT_R), :] = lax.dot_general(g, y, tn, preferred_element_type=F32)
            return c

        lax.fori_loop(0, FFT_K1, mid, 0, unroll=13)

        def inv2(grp, c):
            n2 = grp * FFT_NG
            bn = jnp.concatenate(
                [jnp.concatenate([a_scr[pl.ds(n2 + j, FFT_K1P, stride=A_PITCH), :],
                                  a_scr[pl.ds(FFT_R + n2 + j, FFT_K1P, stride=A_PITCH), :]], axis=0)
                 for j in range(FFT_NG)], axis=1).astype(BF16)
            y = jnp.dot(f1i, bn, preferred_element_type=F32)
            for j in range(FFT_NG):
                u_scr[pl.ds(n2 + j, FFT_H, stride=U_PITCH), :] = y[:, j * LANES:(j + 1) * LANES]
            return c

        lax.fori_loop(0, FFT_R // FFT_NG, inv2, 0, unroll=4)

        cw, cb = cw_ref[...], cb_ref[...]

        def gate(n1, c):
            ub = pl.multiple_of(n1 * U_PITCH, 8)
            z = _conv3_chunk(gate_ref, n1, n_chunks, cw, cb) * u_scr[pl.ds(ub, FFT_R), :]
            if order == HY_ORDER - 1:
                o_ref[pl.ds(pl.multiple_of(n1 * FFT_R, FFT_R), FFT_R), :] = z.astype(o_ref.dtype)
            else:
                u_scr[pl.ds(ub, FFT_R), :] = z
            return c

        lax.fori_loop(0, n_chunks, gate, 0)


def _dft_tables():
    n = FFT_R * FFT_R
    k1 = np.arange(FFT_K1)[:, None]
    n1 = np.arange(FFT_H)[None, :]
    th = 2.0 * np.pi * k1 * n1 / FFT_R
    f1 = np.zeros((2 * FFT_K1P, FFT_H), np.float64)
    f1[:FFT_K1] = np.cos(th)
    f1[FFT_K1P:FFT_K1P + FFT_K1] = -np.sin(th)
    wgt = np.full((FFT_K1, 1), 2.0)
    wgt[0] = 1.0
    wgt[-1] = 1.0
    f1i = np.zeros((FFT_H, 2 * FFT_K1P), np.float64)
    f1i[:, :FFT_K1] = (wgt * np.cos(th)).T / n
    f1i[:, FFT_K1P:FFT_K1P + FFT_K1] = (-wgt * np.sin(th)).T / n
    kk = np.arange(FFT_K1)[:, None, None] + FFT_R * np.arange(FFT_R)[None, :, None]
    n2 = np.arange(FFT_R)[None, None, :]
    ph = 2.0 * np.pi * ((kk * n2) % n) / n
    gr, gi = np.cos(ph), -np.sin(ph)
    g = np.concatenate([np.concatenate([gr, -gi], axis=2), np.concatenate([gi, gr], axis=2)], axis=1)
    return (jnp.asarray(f1, F32).astype(BF16), jnp.asarray(g, F32).astype(BF16),
            jnp.asarray(f1i, F32).astype(BF16))


def _hyena_latent(p, conv_w, conv_b, kf):
    b, s, _ = p.shape
    assert s == FFT_H * FFT_R
    cb_ = LANES
    ncb = HY_W // cb_
    f1, g, f1i = _dft_tables()
    conv_b2 = conv_b.reshape(1, -1)
    seq = lambda off: pl.BlockSpec((None, s, cb_), lambda c, i: (i, 0, off + c))
    cws = lambda off: pl.BlockSpec((3, cb_), lambda c, i: (0, off + c))
    cbs = lambda off: pl.BlockSpec((1, cb_), lambda c, i: (0, off + c))
    return pl.pallas_call(
        _hyena_kernel,
        grid=(ncb, b),
        in_specs=[
            seq(0), seq(ncb), seq(2 * ncb),
            cws(0), cws(ncb), cws(2 * ncb),
            cbs(0), cbs(ncb), cbs(2 * ncb),
            _resident(f1.shape, lambda c, i: (0, 0)),
            _resident(g.shape, lambda c, i: (0, 0, 0)),
            _resident(f1i.shape, lambda c, i: (0, 0)),
            _resident((HY_ORDER, FFT_K1, 2 * FFT_R, cb_), lambda c, i: (0, 0, 0, c)),
        ],
        out_specs=pl.BlockSpec((None, s, cb_), lambda c, i: (i, 0, c)),
        out_shape=jax.ShapeDtypeStruct((b, s, HY_W), BF16),
        scratch_shapes=[
            pltpu.VMEM((FFT_H * U_PITCH, cb_), F32),
            pltpu.VMEM((FFT_K1P * A_PITCH, cb_), F32),
        ],
        compiler_params=_cparams(("arbitrary", "arbitrary")),
        name="hyena_latent",
    )(p, p, p, conv_w, conv_w, conv_w, conv_b2, conv_b2, conv_b2, f1, g, f1i, kf)


def _hyena_ctx_kernel(u_ref, cw_ref, cb_ref, fc_ref, fi_ref, kf_ref, o_ref):
    l = u_ref.shape[0]
    half = fc_ref.shape[0] // 2
    u = u_ref[...].astype(F32)
    row = lax.broadcasted_iota(jnp.int32, u.shape, 0)
    cw = cw_ref[...]
    um1 = jnp.where(row == 0, 0.0, pltpu.roll(u, 1, 0))
    up1 = jnp.where(row == l - 1, 0.0, pltpu.roll(u, l - 1, 0))
    uc = cw[0:1] * um1 + cw[1:2] * u + cw[2:3] * up1 + cb_ref[...]
    z = uc[:, :HY_W]
    hi = lax.Precision.HIGHEST
    for order in range(HY_ORDER):
        gate = uc[:, (order + 1) * HY_W:(order + 2) * HY_W]
        xf = jnp.dot(fc_ref[...], z, precision=hi, preferred_element_type=F32)
        kf = kf_ref[order]
        xr, xi = xf[:half], xf[half:]
        kr, ki = kf[:half], kf[half:]
        y = jnp.concatenate([xr * kr - xi * ki, xr * ki + xi * kr], axis=0)
        z = gate * jnp.dot(fi_ref[...], y, precision=hi, preferred_element_type=F32)
    o_ref[...] = z.astype(o_ref.dtype)


def _ctx_dft_tables(l):
    n = 2 * l
    half = l + 8
    k = np.arange(l + 1)[:, None]
    t = np.arange(l)[None, :]
    th = 2.0 * np.pi * ((k * t) % n) / n
    fc = np.zeros((2 * half, l), np.float64)
    fc[:l + 1] = np.cos(th)
    fc[half:half + l + 1] = -np.sin(th)
    wgt = np.full((l + 1, 1), 2.0)
    wgt[0] = 1.0
    wgt[-1] = 1.0
    fi = np.zeros((l, 2 * half), np.float64)
    fi[:, :l + 1] = (wgt * np.cos(th)).T / n
    fi[:, half:half + l + 1] = (-wgt * np.sin(th)).T / n
    return jnp.asarray(fc, F32), jnp.asarray(fi, F32)


def _hyena_ctx(p, conv_w, conv_b, kf, fc, fi):
    b, l, _ = p.shape
    half = fc.shape[0] // 2
    wtot = 3 * HY_W
    return pl.pallas_call(
        _hyena_ctx_kernel,
        grid=(b,),
        in_specs=[
            pl.BlockSpec((None, l, wtot), lambda i: (i, 0, 0)),
            pl.BlockSpec((3, wtot), lambda i: (0, 0)),
            pl.BlockSpec((1, wtot), lambda i: (0, 0)),
            pl.BlockSpec((2 * half, l), lambda i: (0, 0)),
            pl.BlockSpec((l, 2 * half), lambda i: (0, 0)),
            pl.BlockSpec((HY_ORDER, 2 * half, HY_W), lambda i: (0, 0, 0)),
        ],
        out_specs=pl.BlockSpec((None, l, HY_W), lambda i: (i, 0, 0)),
        out_shape=jax.ShapeDtypeStruct((b, l, HY_W), BF16),
        compiler_params=_cparams(("parallel",)),
        name="hyena_ctx",
    )(p, conv_w, conv_b.reshape(1, -1), fc, fi, kf)


def _mix_ffn_kernel(*refs, recurrent):
    if recurrent:
        (x_ref, z_ref, hf_ref, hr_ref, gx_ref, wo_ref, g1_ref, gain_ref, sc_ref, sh_ref, g2_ref,
         w1_ref, w3_ref, w2_ref, o_ref, x1_scr, h_scr, t_scr) = refs
    else:
        (x_ref, m_ref, wo_ref, g1_ref, gain_ref, sc_ref, sh_ref, g2_ref,
         w1_ref, w3_ref, w2_ref, o_ref, x1_scr, h_scr, t_scr) = refs
    if recurrent:
        hw = z_ref.shape[-1]
        rec = ((hf_ref[...] + hr_ref[...]) * jax.nn.gelu(gx_ref[...].astype(F32))).astype(BF16)
        mixed = (jnp.dot(z_ref[...], wo_ref[:hw, :], preferred_element_type=F32)
                 + jnp.dot(rec, wo_ref[hw:, :], preferred_element_type=F32))
    else:
        mixed = jnp.dot(m_ref[...], wo_ref[...], preferred_element_type=F32)
    x1 = x_ref[...] + g1_ref[...] * mixed
    x1_scr[...] = x1
    h_scr[...] = _norm_mod(x1, gain_ref[...], sc_ref[...], sh_ref[...]).astype(BF16)

    d, ff = w1_ref.shape
    for c0 in range(0, ff, FFN_HIDDEN_CHUNK):
        cols = slice(c0, min(c0 + FFN_HIDDEN_CHUNK, ff))
        h = h_scr[...]
        a = jnp.dot(h, w1_ref[:, cols], preferred_element_type=F32)
        bgate = jnp.dot(h, w3_ref[:, cols], preferred_element_type=F32)
        t_scr[:, cols] = (jax.nn.silu(a) * bgate).astype(BF16)
    for n0 in range(0, d, FFN_OUT_CHUNK):
        cols = slice(n0, n0 + FFN_OUT_CHUNK)
        y = jnp.dot(t_scr[...], w2_ref[:, cols], preferred_element_type=F32)
        o_ref[:, cols] = x1_scr[:, cols] + g2_ref[:, cols] * y


def _mix_ffn(x, mixer_inputs, wo, g1, gain, sc, sh, g2, w1, w3, w2, tm, recurrent):
    b, s, d = x.shape
    ff = w1.shape[1]
    row = lambda width, col=0: pl.BlockSpec((None, tm, width), lambda i, r: (i, r, col))
    per_b = pl.BlockSpec((None, 1, d), lambda i, r: (i, 0, 0))
    if recurrent:
        z, hf, hr, p = mixer_inputs
        mix_specs = [row(HY_W), row(LRU_W), row(LRU_W), row(LRU_W, (3 * HY_W + LRU_W) // LRU_W)]
        mix_args = [z, hf, hr, p]
    else:
        (m,) = mixer_inputs
        mix_specs = [row(d)]
        mix_args = [m]
    return pl.pallas_call(
        functools.partial(_mix_ffn_kernel, recurrent=recurrent),
        grid=(b, s // tm),
        in_specs=[row(d)] + mix_specs + [
            _resident((d, d), lambda i, r: (0, 0)),
            per_b,
            pl.BlockSpec((1, d), lambda i, r: (0, 0)),
            per_b, per_b, per_b,
            _resident((d, ff), lambda i, r: (0, 0)),
            _resident((d, ff), lambda i, r: (0, 0)),
            _resident((ff, d), lambda i, r: (0, 0)),
        ],
        out_specs=row(d),
        out_shape=jax.ShapeDtypeStruct((b, s, d), F32),
        scratch_shapes=[
            pltpu.VMEM((tm, d), F32),
            pltpu.VMEM((tm, d), BF16),
            pltpu.VMEM((tm, ff), BF16),
        ],
        compiler_params=_cparams(("parallel", "parallel")),
        name="mix_ffn_rec" if recurrent else "mix_ffn_attn",
    )(x, *mix_args, wo, g1, gain, sc, sh, g2, w1, w3, w2)


def _filter_features(l):
    t = jnp.linspace(0.0, 1.0, l, dtype=F32)[:, None]
    bands = jnp.linspace(1e-4, HY_BANDS - 1, HY_BANDS, dtype=F32)
    w = 2.0 * math.pi * jnp.arange(l, dtype=F32)[:, None] / l
    z = jnp.concatenate([t, jnp.cos(bands * w), -jnp.sin(bands * w)], axis=-1)
    return jnp.pad(z, ((0, 0), (0, FILT_EMB_PAD - z.shape[1])))


def _filter_mlp_kernel(z_ref, w1_ref, b1_ref, w2_ref, b2_ref, fr_ref, o_ref):
    hp = lax.Precision.HIGHEST
    fr = fr_ref[...]
    h = jnp.sin(fr * (jnp.dot(z_ref[...], w1_ref[...], precision=hp, preferred_element_type=F32) + b1_ref[...]))
    o_ref[...] = jnp.sin(fr * (jnp.dot(h, w2_ref[...], precision=hp, preferred_element_type=F32) + b2_ref[...]))


def _filter_mlp(l, w1, b1, w2, b2, freq):
    z = _filter_features(l)
    w1p = jnp.pad(w1, ((0, FILT_EMB_PAD - w1.shape[0]), (0, 0)))
    hid = w2.shape[0]
    return pl.pallas_call(
        _filter_mlp_kernel,
        out_shape=jax.ShapeDtypeStruct((l, hid), F32),
        compiler_params=pltpu.CompilerParams(vmem_limit_bytes=VMEM_LIMIT_BYTES),
        name="filter_mlp",
    )(z, w1p, b1[None], w2, b2[None], freq[None])


def _filter_decay():
    max_decay = math.log(HY_DECAY_TARGET) / HY_FAST_DECAY
    min_decay = math.log(HY_DECAY_TARGET) / HY_SLOW_DECAY
    return jnp.abs(jnp.linspace(min_decay, max_decay, HY_W, dtype=F32))[None]


def _filter_taps(h, t, w3f, w3b, decay, row0):
    hp = lax.Precision.HIGHEST
    window = jnp.exp(-t * decay) + HY_MOD_SHIFT
    taps = jnp.dot(h, jnp.concatenate([w3f, w3b], axis=1), precision=hp, preferred_element_type=F32)
    hf = taps[:, :LANES] * window
    hb = taps[:, LANES:] * window
    row = lax.broadcasted_iota(jnp.int32, hb.shape, 0) + row0
    return hf, jnp.where(row == 0, 0.0, hb)


def _filter_spec_latent_kernel(h_ref, t_ref, w3f_ref, w3b_ref, dec_ref, bias_ref, f1_ref, g_ref,
                               o_ref, uf_scr, ub_scr, af_scr, ab_scr):
    w3f, w3b, dec = w3f_ref[...], w3b_ref[...], dec_ref[...]

    def load(n1, acc):
        base = pl.multiple_of(n1 * FFT_R, FFT_R)
        hf, hb = _filter_taps(h_ref[pl.ds(base, FFT_R), :], t_ref[pl.ds(base, FFT_R), :], w3f, w3b, dec, base)
        ub = pl.multiple_of(n1 * U_PITCH, 8)
        uf_scr[pl.ds(ub, FFT_R), :] = hf
        ub_scr[pl.ds(ub, FFT_R), :] = hb
        return acc + jnp.sum(jnp.abs(hf) + jnp.abs(hb), axis=0, keepdims=True)

    l1 = lax.fori_loop(0, FFT_H, load, jnp.zeros((1, LANES), F32), unroll=4)
    inv = 1.0 / l1
    bias = bias_ref[...]
    f1 = f1_ref[...]
    _fwd_level1(uf_scr, af_scr, f1)
    _fwd_level1(ub_scr, ab_scr, f1)

    def level2(k1, c):
        base = pl.multiple_of(k1 * A_PITCH, 8)
        g = g_ref[k1]
        xf = jnp.dot(g, af_scr[pl.ds(base, 2 * FFT_R), :].astype(BF16), preferred_element_type=F32)
        xb = jnp.dot(g, ab_scr[pl.ds(base, 2 * FFT_R), :].astype(BF16), preferred_element_type=F32)
        kr = (xf[:FFT_R] + xb[:FFT_R]) * inv + bias
        ki = (xf[FFT_R:] - xb[FFT_R:]) * inv
        o_ref[k1] = jnp.concatenate([kr, ki], axis=0).astype(o_ref.dtype)
        return c

    lax.fori_loop(0, FFT_K1, level2, 0, unroll=5)


def _filter_spec_latent(h, w3, bias):
    l, hid = h.shape
    assert l == FFT_H * FFT_R
    f1, g, _ = _dft_tables()
    ncb = HY_W // LANES
    t = jnp.linspace(0.0, 1.0, l, dtype=F32)[:, None]
    return pl.pallas_call(
        _filter_spec_latent_kernel,
        grid=(HY_ORDER * ncb,),
        in_specs=[
            pl.BlockSpec((l, hid), lambda i: (0, 0)),
            pl.BlockSpec((l, 1), lambda i: (0, 0)),
            pl.BlockSpec((hid, LANES), lambda i: (0, i)),
            pl.BlockSpec((hid, LANES), lambda i: (0, HY_ORDER * ncb + i)),
            pl.BlockSpec((1, LANES), lambda i: (0, i % ncb)),
            pl.BlockSpec((None, 1, LANES), lambda i: (i // ncb, 0, i % ncb)),
            _resident(f1.shape, lambda i: (0, 0)),
            _resident(g.shape, lambda i: (0, 0, 0)),
        ],
        out_specs=pl.BlockSpec((None, FFT_K1, 2 * FFT_R, LANES), lambda i: (i // ncb, 0, 0, i % ncb)),
        out_shape=jax.ShapeDtypeStruct((HY_ORDER, FFT_K1, 2 * FFT_R, HY_W), BF16),
        scratch_shapes=[
            pltpu.VMEM((FFT_H * U_PITCH, LANES), F32),
            pltpu.VMEM((FFT_H * U_PITCH, LANES), F32),
            pltpu.VMEM((FFT_K1P * A_PITCH, LANES), F32),
            pltpu.VMEM((FFT_K1P * A_PITCH, LANES), F32),
        ],
        compiler_params=_cparams(("arbitrary",)),
        name="filter_spec_latent",
    )(h, t, w3, w3, _filter_decay(), bias[:, None, :], f1, g)


def _filter_spec_ctx_kernel(h_ref, t_ref, w3f_ref, w3b_ref, dec_ref, bias_ref, fc_ref, o_ref):
    hp = lax.Precision.HIGHEST
    half = fc_ref.shape[0] // 2
    hf, hb = _filter_taps(h_ref[...], t_ref[...], w3f_ref[...], w3b_ref[...], dec_ref[...], 0)
    inv = 1.0 / jnp.sum(jnp.abs(hf) + jnp.abs(hb), axis=0, keepdims=True)
    xf = jnp.dot(fc_ref[...], hf, precision=hp, preferred_element_type=F32)
    xb = jnp.dot(fc_ref[...], hb, precision=hp, preferred_element_type=F32)
    kr = (xf[:half] + xb[:half]) * inv + bias_ref[...]
    ki = (xf[half:] - xb[half:]) * inv
    o_ref[...] = jnp.concatenate([kr, ki], axis=0)


def _filter_spec_ctx(h, w3, bias, fc):
    l, hid = h.shape
    ncb = HY_W // LANES
    t = jnp.linspace(0.0, 1.0, l, dtype=F32)[:, None]
    return pl.pallas_call(
        _filter_spec_ctx_kernel,
        grid=(HY_ORDER * ncb,),
        in_specs=[
            pl.BlockSpec((l, hid), lambda i: (0, 0)),
            pl.BlockSpec((l, 1), lambda i: (0, 0)),
            pl.BlockSpec((hid, LANES), lambda i: (0, i)),
            pl.BlockSpec((hid, LANES), lambda i: (0, HY_ORDER * ncb + i)),
            pl.BlockSpec((1, LANES), lambda i: (0, i % ncb)),
            pl.BlockSpec((None, 1, LANES), lambda i: (i // ncb, 0, i % ncb)),
            pl.BlockSpec(fc.shape, lambda i: (0, 0)),
        ],
        out_specs=pl.BlockSpec((None, fc.shape[0], LANES), lambda i: (i // ncb, 0, i % ncb)),
        out_shape=jax.ShapeDtypeStruct((HY_ORDER, fc.shape[0], HY_W), F32),
        compiler_params=_cparams(("parallel",)),
        name="filter_spec_ctx",
    )(h, t, w3, w3, _filter_decay(), bias[:, None, :], fc)


def _mod_kernel(c_ref, w_ref, b_ref, o_ref):
    act = jax.nn.silu(c_ref[...])
    o_ref[...] = jnp.dot(act, w_ref[...], precision=lax.Precision.HIGHEST,
                         preferred_element_type=F32) + b_ref[...]


def _modulations(c, c_ctx, w_mod, b_mod):
    depth, d, n = w_mod.shape
    nb = c.shape[0]
    rows = 2 * nb
    cc = jnp.zeros((rows, d), F32).at[:nb].set(c).at[nb].set(c_ctx)
    tn = n // 4
    return pl.pallas_call(
        _mod_kernel,
        grid=(depth, n // tn),
        in_specs=[
            pl.BlockSpec((rows, d), lambda i, j: (0, 0)),
            pl.BlockSpec((None, d, tn), lambda i, j: (i, 0, j)),
            pl.BlockSpec((None, 1, tn), lambda i, j: (i, 0, j)),
        ],
        out_specs=pl.BlockSpec((None, rows, tn), lambda i, j: (i, 0, j)),
        out_shape=jax.ShapeDtypeStruct((depth, rows, n), F32),
        compiler_params=_cparams(("parallel", "parallel")),
        name="adaln_mod",
    )(cc, w_mod, b_mod[:, None, :])


def _block_diag(w):
    h, bw, _ = w.shape
    eye = jnp.eye(h, dtype=w.dtype)
    return (eye[:, None, :, None] * w[:, :, None, :]).reshape(h * bw, h * bw)


def _dup_kv_columns(w):
    nq = N_HEADS * HEAD_DIM
    d = w.shape[0]
    kv = w[:, nq:].reshape(d, 2 * N_KV, 1, HEAD_DIM)
    kv = jnp.broadcast_to(kv, (d, 2 * N_KV, LANES // HEAD_DIM, HEAD_DIM)).reshape(d, -1)
    return jnp.concatenate([w[:, :nq], kv], axis=1)


def _split_mod(m):
    return [t[:, None, :] for t in jnp.split(m, 6, axis=-1)]


def _rope_tables(s):
    half = HEAD_DIM // 2
    nf = half // 2
    inv = jnp.power(ROPE_BASE, -jnp.arange(nf, dtype=F32) / nf)
    pos = jnp.arange(s, dtype=jnp.int32)
    row = (pos // GRID_W).astype(F32)[:, None] * inv
    col = (pos % GRID_W).astype(F32)[:, None] * inv
    cos = jnp.concatenate([jnp.cos(row), jnp.cos(row), jnp.cos(col), jnp.cos(col)], axis=-1)
    sin = jnp.concatenate([-jnp.sin(row), jnp.sin(row), -jnp.sin(col), jnp.sin(col)], axis=-1)
    return jnp.tile(cos, (1, LANES // HEAD_DIM)), jnp.tile(sin, (1, LANES // HEAD_DIM))


def _rope_gain_tables(cos, sin, gain, scale):
    g = jnp.tile(gain, LANES // HEAD_DIM)
    first_half = (jnp.arange(LANES) % 32) < 16
    g_partner = jnp.where(first_half, jnp.roll(g, -16), jnp.roll(g, 16))
    return cos * (g * scale)[None], sin * (g_partner * scale)[None]


def kernel(x, c, ctx, c_ctx, norm1, norm2, w_mod, b_mod, ffn_w1, ffn_w3, ffn_w2, ab_w_in, hy_conv_w, hy_conv_b, hy_f_w1, hy_f_b1, hy_f_w2, hy_f_b2, hy_f_w3, hy_f_freq, hy_bias, lru_conv_w, lru_conv_b, lru_w_a, lru_b_a, lru_w_i, lru_b_i, lru_lam, ab_w_out, at_w_qkv, at_q_gain, at_k_gain, at_sink, at_w_o):
    nb, s, d = x.shape
    cl = ctx.shape[1]
    bf = lambda a: a.astype(BF16)

    mods = _modulations(c, c_ctx, w_mod, b_mod)
    sh1x, sc1x, g1x, sh2x, sc2x, g2x = _split_mod(mods[0, :nb])
    sh1c, sc1c, g1c, sh2c, sc2c, g2c = _split_mod(jnp.broadcast_to(mods[0, nb:nb + 1], (nb, mods.shape[-1])))
    gain1 = norm1[0][None]
    gain2 = norm2[0][None]
    w_in = bf(ab_w_in[0])
    tm_x, tm_c = min(ROW_TILE, s), min(ROW_TILE, cl)
    px = _proj(x, gain1, sc1x, sh1x, w_in, tm_x)
    pc = _proj(ctx, gain1, sc1c, sh1c, w_in, tm_c)

    mlp = (hy_f_w1[0], hy_f_b1[0], hy_f_w2[0], hy_f_b2[0], hy_f_freq[0])
    fc, fi = _ctx_dft_tables(cl)
    kf_x = _filter_spec_latent(_filter_mlp(s, *mlp), hy_f_w3[0], hy_bias[0])
    kf_c = _filter_spec_ctx(_filter_mlp(cl, *mlp), hy_f_w3[0], hy_bias[0], fc)
    zx = _hyena_latent(px, hy_conv_w[0], hy_conv_b[0], kf_x)
    zc = _hyena_ctx(pc, hy_conv_w[0], hy_conv_b[0], kf_c, fc, fi)

    lru_col = 3 * HY_W // LRU_W
    lcb = lru_conv_b[0][None]
    h0 = jnp.zeros((nb, LRU_W), F32)
    hx, hc = [], []
    for dr, reverse in enumerate((False, True)):
        wg = bf(0.5 * jnp.concatenate([_block_diag(lru_w_a[0, dr]), _block_diag(lru_w_i[0, dr])], axis=1))
        bg = 0.5 * jnp.concatenate([lru_b_a[0, dr], lru_b_i[0, dr]])[None]
        lam = lru_lam[0, dr][None]
        hcs, h_end = _lru(pc, lru_col, lru_conv_w[0], lcb, wg, bg, lam, h0, reverse, min(LRU_TIME_CHUNK, cl))
        hxs, _ = _lru(px, lru_col, lru_conv_w[0], lcb, wg, bg, lam, h_end, reverse, min(LRU_TIME_CHUNK, s))
        hx.append(hxs)
        hc.append(hcs)

    w_out = bf(ab_w_out[0])
    w1, w3, w2 = bf(ffn_w1[0]), bf(ffn_w3[0]), bf(ffn_w2[0])
    x = _mix_ffn(x, (zx, hx[0], hx[1], px), w_out, g1x, gain2, sc2x, sh2x, g2x, w1, w3, w2, tm_x, True)
    ctx = _mix_ffn(ctx, (zc, hc[0], hc[1], pc), w_out, g1c, gain2, sc2c, sh2c, g2c, w1, w3, w2, tm_c, True)

    sh1x, sc1x, g1x, sh2x, sc2x, g2x = _split_mod(mods[1, :nb])
    sh1c, sc1c = _split_mod(jnp.broadcast_to(mods[1, nb:nb + 1], (nb, mods.shape[-1])))[:2]
    gain1 = norm1[1][None]
    gain2 = norm2[1][None]
    w_qkv = bf(_dup_kv_columns(at_w_qkv[0]))
    cos, sin = _rope_tables(s)
    q_scale = HEAD_DIM ** -0.5
    q, k, v = _qkv(x, gain1, sc1x, sh1x, w_qkv,
                   *_rope_gain_tables(cos, sin, at_q_gain[0], q_scale),
                   *_rope_gain_tables(cos, sin, at_k_gain[0], 1.0), tm_x)
    no_rot = (jnp.ones((cl, LANES), F32), jnp.zeros((cl, LANES), F32))
    _, kx, vx = _qkv(ctx, gain1, sc1c, sh1c, w_qkv,
                     *_rope_gain_tables(*no_rot, at_q_gain[0], q_scale),
                     *_rope_gain_tables(*no_rot, at_k_gain[0], 1.0), tm_c)
    o = _attention(q, k, v, kx, vx, at_sink[0], WINDOW)
    w1, w3, w2 = bf(ffn_w1[1]), bf(ffn_w3[1]), bf(ffn_w2[1])
    return _mix_ffn(x, (o,), bf(at_w_o[0]), g1x, gain2, sc2x, sh2x, g2x, w1, w3, w2, tm_x, False)
```

```python
import functools
import math

import numpy as np
import jax
import jax.numpy as jnp
from jax import lax
from jax.experimental import pallas as pl
from jax.experimental.pallas import tpu as pltpu

F32 = jnp.float32
BF16 = jnp.bfloat16

EPS = 1e-6
D_MODEL = 1024
GRID_W = 64
HY_W = 512
HY_ORDER = 2
HY_BANDS = 16
HY_MOD_SHIFT = 0.05
HY_FAST_DECAY = 0.3
HY_SLOW_DECAY = 1.5
HY_DECAY_TARGET = 1e-2
LRU_W = 512
LRU_HEADS = 8
LRU_C = 8.0
HEAD_DIM = 64
N_HEADS = 16
N_KV = 4
GROUP = N_HEADS // N_KV
WINDOW = 128
ROPE_BASE = 10000.0
NEG_INF = -1e30
TINY = 1e-30

VMEM_LIMIT_BYTES = 58 * 1024 * 1024
LANES = 128

FFT_R = 128
FFT_H = FFT_R // 2
FFT_K1 = FFT_R // 2 + 1
FFT_K1P = 72
FFT_NG = 4
CONV_HALO = 16
ROW_TILE = 512
LRU_TIME_CHUNK = 256
ATTN_QBLOCKS = 2
FFN_HIDDEN_CHUNK = 256
FFN_OUT_CHUNK = 256
FILT_EMB_PAD = 40
U_PITCH = FFT_R + 8
A_PITCH = 2 * FFT_R + 8


def _cparams(sem):
    return pltpu.CompilerParams(dimension_semantics=sem, vmem_limit_bytes=VMEM_LIMIT_BYTES)


def _resident(block_shape, index_map):
    return pl.BlockSpec(block_shape, index_map, pipeline_mode=pl.Buffered(1))


def _norm_mod(x, gain, scale, shift):
    y = x * lax.rsqrt(jnp.mean(x * x, axis=-1, keepdims=True) + EPS)
    return (y * gain) * (1.0 + scale) + shift


def _proj_kernel(x_ref, gain_ref, sc_ref, sh_ref, w_ref, o_ref, *, n_chunk):
    h = _norm_mod(x_ref[...], gain_ref[...], sc_ref[...], sh_ref[...]).astype(BF16)
    n_out = o_ref.shape[-1]
    for n0 in range(0, n_out, n_chunk):
        o_ref[:, n0:n0 + n_chunk] = jnp.dot(
            h, w_ref[:, n0:n0 + n_chunk], preferred_element_type=F32).astype(o_ref.dtype)


def _proj(x, gain, sc, sh, w, tm):
    b, s, d = x.shape
    n = w.shape[1]
    return pl.pallas_call(
        functools.partial(_proj_kernel, n_chunk=512),
        grid=(b, s // tm),
        in_specs=[
            pl.BlockSpec((None, tm, d), lambda i, j: (i, j, 0)),
            pl.BlockSpec((1, d), lambda i, j: (0, 0)),
            pl.BlockSpec((None, 1, d), lambda i, j: (i, 0, 0)),
            pl.BlockSpec((None, 1, d), lambda i, j: (i, 0, 0)),
            _resident((d, n), lambda i, j: (0, 0)),
        ],
        out_specs=pl.BlockSpec((None, tm, n), lambda i, j: (i, j, 0)),
        out_shape=jax.ShapeDtypeStruct((b, s, n), BF16),
        compiler_params=_cparams(("parallel", "parallel")),
        name="proj_in",
    )(x, gain, sc, sh, w)


def _qkv_kernel(x_ref, gain_ref, sc_ref, sh_ref, w_ref, hm_ref, cq_ref, sq_ref, ck_ref, sk_ref,
                q_ref, k_ref, v_ref):
    tm = x_ref.shape[0]
    nq = q_ref.shape[-1]
    nkv = k_ref.shape[-1]
    wide = 2 * LANES
    h = _norm_mod(x_ref[...], gain_ref[...], sc_ref[...], sh_ref[...]).astype(BF16)
    hm = hm_ref[...]
    lane = lax.broadcasted_iota(jnp.int32, (tm, wide), 1)
    first_half = (lane % 32) < 16

    def norm_rope(y, cos_g, sin_g):
        inv = lax.rsqrt(jnp.dot((y * y).astype(BF16), hm, preferred_element_type=F32) + EPS)
        partner = jnp.where(first_half, pltpu.roll(y, wide - 16, 1), pltpu.roll(y, 16, 1))
        return (y * cos_g + partner * sin_g) * inv

    def tables(c_ref, s_ref):
        c, s = c_ref[...], s_ref[...]
        return jnp.concatenate([c, c], axis=1), jnp.concatenate([s, s], axis=1)

    cq, sq = tables(cq_ref, sq_ref)
    for n0 in range(0, nq, 512):
        y = jnp.dot(h, w_ref[:, n0:n0 + 512], preferred_element_type=F32)
        for j in range(512 // wide):
            q_ref[:, n0 + j * wide:n0 + (j + 1) * wide] = norm_rope(
                y[:, j * wide:(j + 1) * wide], cq, sq).astype(BF16)
    ck, sk = tables(ck_ref, sk_ref)
    y = jnp.dot(h, w_ref[:, nq:nq + nkv], preferred_element_type=F32)
    for j in range(nkv // wide):
        k_ref[:, j * wide:(j + 1) * wide] = norm_rope(y[:, j * wide:(j + 1) * wide], ck, sk).astype(BF16)
    v_ref[...] = jnp.dot(h, w_ref[:, nq + nkv:], preferred_element_type=F32).astype(BF16)


def _head_mean_matrix():
    head = np.arange(2 * LANES) // HEAD_DIM
    return jnp.asarray((head[:, None] == head[None, :]) / HEAD_DIM, F32).astype(BF16)


def _qkv(x, gain, sc, sh, w, cq, sq, ck, sk, tm):
    b, s, d = x.shape
    nq = N_HEADS * HEAD_DIM
    nkv = (w.shape[1] - nq) // 2
    tab = pl.BlockSpec((tm, LANES), lambda i, j: (j, 0))
    return pl.pallas_call(
        _qkv_kernel,
        grid=(b, s // tm),
        in_specs=[
            pl.BlockSpec((None, tm, d), lambda i, j: (i, j, 0)),
            pl.BlockSpec((1, d), lambda i, j: (0, 0)),
            pl.BlockSpec((None, 1, d), lambda i, j: (i, 0, 0)),
            pl.BlockSpec((None, 1, d), lambda i, j: (i, 0, 0)),
            _resident((d, nq + 2 * nkv), lambda i, j: (0, 0)),
            pl.BlockSpec((2 * LANES, 2 * LANES), lambda i, j: (0, 0)),
            tab, tab, tab, tab,
        ],
        out_specs=[
            pl.BlockSpec((None, tm, nq), lambda i, j: (i, j, 0)),
            pl.BlockSpec((None, tm, nkv), lambda i, j: (i, j, 0)),
            pl.BlockSpec((None, tm, nkv), lambda i, j: (i, j, 0)),
        ],
        out_shape=[
            jax.ShapeDtypeStruct((b, s, nq), BF16),
            jax.ShapeDtypeStruct((b, s, nkv), BF16),
            jax.ShapeDtypeStruct((b, s, nkv), BF16),
        ],
        compiler_params=_cparams(("parallel", "parallel")),
        name="qkv",
    )(x, gain, sc, sh, w, _head_mean_matrix(), cq, sq, ck, sk)


def _attn_kernel(sink_ref, bias_a_ref, bias_b_ref, q_ref, k0_ref, k1_ref, k2_ref, k3_ref,
                 v0_ref, v1_ref, v2_ref, v3_ref, kx_ref, vx_ref, o_ref):
    blk = k0_ref.shape[0]
    n_loc = 3 * blk
    n_keys = n_loc + kx_ref.shape[0]
    k_refs = (k0_ref, k1_ref, k2_ref, k3_ref)
    v_refs = (v0_ref, v1_ref, v2_ref, v3_ref)
    lane = lax.broadcasted_iota(jnp.int32, (blk, LANES), 1)
    low = lane < HEAD_DIM
    keep_low = jnp.where(low, 1.0, 0.0).astype(BF16)
    keep_high = jnp.where(low, 0.0, 1.0).astype(BF16)
    ones = jnp.ones((n_keys, LANES), BF16)
    nt = (((1,), (1,)), ((), ()))
    for qb, bias_ref in enumerate((bias_a_ref, bias_b_ref)):
        rows = slice(qb * blk, (qb + 1) * blk)
        bias = bias_ref[...]
        for g in range(N_KV):
            gl = slice(g * LANES, (g + 1) * LANES)
            k_all = jnp.concatenate([r[:, gl] for r in k_refs[qb:qb + 3]] + [kx_ref[:, gl]], axis=0)
            v_all = jnp.concatenate([r[:, gl] for r in v_refs[qb:qb + 3]] + [vx_ref[:, gl]], axis=0)
            v_aug = jnp.concatenate([v_all, ones], axis=1)
            q4 = jnp.concatenate(
                [q_ref[rows, ((GROUP * g + j) // 2) * LANES:((GROUP * g + j) // 2 + 1) * LANES]
                 * (keep_low if j % 2 == 0 else keep_high) for j in range(GROUP)], axis=0)
            s = lax.dot_general(q4, k_all, nt, preferred_element_type=F32)
            s_loc = jnp.concatenate([s[:, :blk] + bias[:, :blk], s[:, blk:2 * blk],
                                     s[:, 2 * blk:n_loc] + bias[:, 2 * blk:]], axis=1)
            s_ctx = s[:, n_loc:]
            s_sink = jnp.concatenate(
                [jnp.full((blk, 1), sink_ref[GROUP * g + j], F32) for j in range(GROUP)], axis=0)
            m = jnp.maximum(jnp.maximum(jnp.max(s_loc, axis=-1, keepdims=True),
                                        jnp.max(s_ctx, axis=-1, keepdims=True)), s_sink)
            p = jnp.exp(jnp.concatenate([s_loc - m, s_ctx - m], axis=1).astype(BF16))
            pv = jnp.dot(p, v_aug, preferred_element_type=F32)
            o4 = pv[:, :LANES] / (pv[:, LANES:] + jnp.exp(s_sink - m))
            for pair in range(GROUP // 2):
                slab = jnp.where(low, o4[2 * pair * blk:(2 * pair + 1) * blk],
                                 o4[(2 * pair + 1) * blk:(2 * pair + 2) * blk])
                col = (GROUP // 2) * g + pair
                o_ref[rows, col * LANES:(col + 1) * LANES] = slab.astype(BF16)


def _attention(q, k, v, kx, vx, sink, blk):
    b, s, nq = q.shape
    nkv = k.shape[-1]
    cx = kx.shape[1]
    nb = s // blk
    steps = nb // ATTN_QBLOCKS
    qspec = pl.BlockSpec((None, ATTN_QBLOCKS * blk, nq), lambda i, j, *_: (i, j, 0))

    def kv_spec(off):
        return pl.BlockSpec(
            (None, blk, nkv), lambda i, j, *_: (i, jnp.clip(ATTN_QBLOCKS * j + off, 0, nb - 1), 0))

    kv_specs = [kv_spec(off) for off in range(-1, ATTN_QBLOCKS + 1)]
    cxs = pl.BlockSpec((None, cx, nkv), lambda i, j, *_: (i, 0, 0))
    r = np.arange(GROUP * blk)[:, None] % blk
    c = np.arange(3 * blk)[None, :]
    band = np.abs(c - blk - r) <= WINDOW
    variants = [band & (c >= blk), band, band & (c < 2 * blk)]
    bias = jnp.asarray(np.where(np.stack(variants), 0.0, NEG_INF), F32)
    bias_shape = (None, GROUP * blk, 3 * blk)
    bias_first = pl.BlockSpec(bias_shape, lambda i, j, *_: (jnp.where(j == 0, 0, 1), 0, 0))
    bias_last = pl.BlockSpec(bias_shape, lambda i, j, *_: (jnp.where(j == steps - 1, 2, 1), 0, 0))
    return pl.pallas_call(
        _attn_kernel,
        grid_spec=pltpu.PrefetchScalarGridSpec(
            num_scalar_prefetch=1,
            grid=(b, steps),
            in_specs=[bias_first, bias_last, qspec] + kv_specs + kv_specs + [cxs, cxs],
            out_specs=pl.BlockSpec((None, ATTN_QBLOCKS * blk, nq), lambda i, j, *_: (i, j, 0)),
        ),
        out_shape=jax.ShapeDtypeStruct((b, s, nq), BF16),
        compiler_params=_cparams(("parallel", "parallel")),
        name="window_attn",
    )(sink, bias, bias, q, k, k, k, k, v, v, v, v, kx, vx)


def _lru_kernel(prev_ref, cur_ref, next_ref, cw_ref, cb_ref, wg_ref, bg_ref, lam_ref, h0_ref,
                h_ref, hend_ref, a_scr, b_scr, carry, *, reverse, n_chunks):
    nb, t, w = cur_ref.shape
    pitch = t + 8
    n_lane_groups = w // LANES
    i = pl.program_id(0)
    chunk = (n_chunks - 1 - i) if reverse else i

    @pl.when(i == 0)
    def _():
        carry[...] = h0_ref[...]

    lam = lam_ref[...]
    softplus_neg_lam = jnp.maximum(-lam, 0.0) + jnp.log1p(jnp.exp(-jnp.abs(lam)))
    neg_half_c_softplus = (-0.5 * LRU_C) * softplus_neg_lam
    cw = cw_ref[...]
    cb = cb_ref[...]
    bg = bg_ref[...]
    row = lax.broadcasted_iota(jnp.int32, (t, w), 0)
    has_prev = chunk > 0
    has_next = chunk < n_chunks - 1
    for b in range(nb):
        cur = cur_ref[b].astype(F32)
        pv = prev_ref[b].astype(F32)
        nx = next_ref[b].astype(F32)
        pm2 = jnp.where(has_prev, pv[14:15], 0.0)
        pm1 = jnp.where(has_prev, pv[15:16], 0.0)
        nx0 = jnp.where(has_next, nx[0:1], 0.0)
        xm1 = jnp.where(row == 0, pm1, pltpu.roll(cur, 1, 0))
        xm2 = jnp.where(row == 0, pm2, jnp.where(row == 1, pm1, pltpu.roll(cur, 2, 0)))
        xp1 = jnp.where(row == t - 1, nx0, pltpu.roll(cur, t - 1, 0))
        x = cw[0:1] * xm2 + cw[1:2] * xm1 + cw[2:3] * cur + cw[3:4] * xp1 + cb
        tg = jnp.tanh(jnp.dot(x.astype(BF16), wg_ref[...], preferred_element_type=F32) + bg)
        log_a = neg_half_c_softplus * (1.0 + tg[:, :w])
        a = jnp.exp(log_a)
        z = 1.0 - a * a
        root = z * lax.rsqrt(jnp.maximum(z, TINY))
        bb = root * ((1.0 + tg[:, w:]) * (0.5 * x))
        for gi in range(n_lane_groups):
            a_scr[gi, pl.ds(b * pitch, t), :] = a[:, gi * LANES:(gi + 1) * LANES]
            b_scr[gi, pl.ds(b * pitch, t), :] = bb[:, gi * LANES:(gi + 1) * LANES]

    def step(s, hs):
        tt = (t - 1 - s) if reverse else s
        out = []
        for gi in range(n_lane_groups):
            a_t = a_scr[gi, pl.ds(tt, nb, stride=pitch), :]
            b_t = b_scr[gi, pl.ds(tt, nb, stride=pitch), :]
            h_new = a_t * hs[gi] + b_t
            b_scr[gi, pl.ds(tt, nb, stride=pitch), :] = h_new
            out.append(h_new)
        return tuple(out)

    h_init = tuple(carry[:, gi * LANES:(gi + 1) * LANES] for gi in range(n_lane_groups))
    h_fin = lax.fori_loop(0, t, step, h_init, unroll=8)
    for gi in range(n_lane_groups):
        carry[:, gi * LANES:(gi + 1) * LANES] = h_fin[gi]
        hend_ref[:, gi * LANES:(gi + 1) * LANES] = h_fin[gi]
    for b in range(nb):
        for gi in range(n_lane_groups):
            h_ref[b, :, gi * LANES:(gi + 1) * LANES] = b_scr[gi, pl.ds(b * pitch, t), :]


def _lru(p, col_block, cw, cb, wg, bg, lam, h0, reverse, t):
    nb, s, _ = p.shape
    w = LRU_W
    n_chunks = s // t
    halo = 16
    tb = t // halo
    last_halo = s // halo - 1
    if reverse:
        cidx = lambda i: n_chunks - 1 - i
    else:
        cidx = lambda i: i
    kern = functools.partial(_lru_kernel, reverse=reverse, n_chunks=n_chunks)
    return pl.pallas_call(
        kern,
        grid=(n_chunks,),
        in_specs=[
            pl.BlockSpec((nb, halo, w), lambda i: (0, jnp.maximum(cidx(i) * tb - 1, 0), col_block)),
            pl.BlockSpec((nb, t, w), lambda i: (0, cidx(i), col_block)),
            pl.BlockSpec((nb, halo, w), lambda i: (0, jnp.minimum((cidx(i) + 1) * tb, last_halo), col_block)),
            pl.BlockSpec((4, w), lambda i: (0, 0)),
            pl.BlockSpec((1, w), lambda i: (0, 0)),
            _resident((w, 2 * w), lambda i: (0, 0)),
            pl.BlockSpec((1, 2 * w), lambda i: (0, 0)),
            pl.BlockSpec((1, w), lambda i: (0, 0)),
            pl.BlockSpec((nb, w), lambda i: (0, 0)),
        ],
        out_specs=[
            pl.BlockSpec((nb, t, w), lambda i: (0, cidx(i), 0)),
            pl.BlockSpec((nb, w), lambda i: (0, 0)),
        ],
        out_shape=[
            jax.ShapeDtypeStruct((nb, s, w), F32),
            jax.ShapeDtypeStruct((nb, w), F32),
        ],
        scratch_shapes=[
            pltpu.VMEM((w // LANES, nb * (t + 8), LANES), F32),
            pltpu.VMEM((w // LANES, nb * (t + 8), LANES), F32),
            pltpu.VMEM((nb, w), F32),
        ],
        compiler_params=_cparams(("arbitrary",)),
        name="rglru_rev" if reverse else "rglru_fwd",
    )(p, p, p, cw, cb, wg, bg, lam, h0)


def _conv3_chunk(ref, n1, n_chunks, w, b):
    rows = FFT_R
    base = pl.multiple_of(n1 * rows, rows)
    cur = ref[pl.ds(base, rows), :].astype(F32)
    pbase = pl.multiple_of(jnp.maximum(base - CONV_HALO, 0), CONV_HALO)
    nbase = pl.multiple_of(jnp.minimum(base + rows, (n_chunks - 1) * rows), CONV_HALO)
    prev = jnp.where(n1 > 0, ref[pl.ds(pbase, CONV_HALO), :].astype(F32)[CONV_HALO - 1:CONV_HALO], 0.0)
    nxt = jnp.where(n1 < n_chunks - 1, ref[pl.ds(nbase, CONV_HALO), :].astype(F32)[0:1], 0.0)
    row = lax.broadcasted_iota(jnp.int32, cur.shape, 0)
    xm1 = jnp.where(row == 0, prev, pltpu.roll(cur, 1, 0))
    xp1 = jnp.where(row == rows - 1, nxt, pltpu.roll(cur, rows - 1, 0))
    return w[0:1] * xm1 + w[1:2] * cur + w[2:3] * xp1 + b


def _fwd_level1(u_scr, a_scr, f1):
    def body(grp, c):
        n2 = grp * FFT_NG
        xs = jnp.concatenate(
            [u_scr[pl.ds(n2 + j, FFT_H, stride=U_PITCH), :] for j in range(FFT_NG)], axis=1).astype(BF16)
        a = jnp.dot(f1, xs, preferred_element_type=F32)
        for j in range(FFT_NG):
            lanes = slice(j * LANES, (j + 1) * LANES)
            a_scr[pl.ds(n2 + j, FFT_K1P, stride=A_PITCH), :] = a[:FFT_K1P, lanes]
            a_scr[pl.ds(FFT_R + n2 + j, FFT_K1P, stride=A_PITCH), :] = a[FFT_K1P:, lanes]
        return c

    lax.fori_loop(0, FFT_R // FFT_NG, body, 0, unroll=8)


def _hyena_kernel(v_ref, x1_ref, x2_ref, cwv_ref, cw1_ref, cw2_ref, cbv_ref, cb1_ref, cb2_ref,
                  f1_ref, g_ref, f1i_ref, kf_ref, o_ref, u_scr, a_scr):
    n_chunks = FFT_H
    cwv, cbv = cwv_ref[...], cbv_ref[...]

    def load_v(n1, c):
        u_scr[pl.ds(pl.multiple_of(n1 * U_PITCH, 8), FFT_R), :] = _conv3_chunk(v_ref, n1, n_chunks, cwv, cbv)
        return c

    lax.fori_loop(0, n_chunks, load_v, 0)
    f1 = f1_ref[...]
    f1i = f1i_ref[...]
    tn = (((0,), (0,)), ((), ()))

    for order, (gate_ref, cw_ref, cb_ref) in enumerate(((x1_ref, cw1_ref, cb1_ref), (x2_ref, cw2_ref, cb2_ref))):
        _fwd_level1(u_scr, a_scr, f1)

        def mid(k1, c):
            base = pl.multiple_of(k1 * A_PITCH, 8)
            g = g_ref[k1]
            ak = a_scr[pl.ds(base, 2 * FFT_R), :].astype(BF16)
            xk = jnp.dot(g, ak, preferred_element_type=F32)
            kf = kf_ref[order, k1].astype(F32)
            xr, xi = xk[:FFT_R], xk[FFT_R:]
            kr, ki = kf[:FFT_R], kf[FFT_R:]
            y = jnp.concatenate([xr * kr - xi * ki, xr * ki + xi * kr], axis=0).astype(BF16)
            a_scr[pl.ds(base, 2 * FFT_R), :] = lax.dot_general(g, y, tn, preferred_element_type=F32)
            return c

        lax.fori_loop(0, FFT_K1, mid, 0, unroll=13)

        def inv2(grp, c):
            n2 = grp * FFT_NG
            bn = jnp.concatenate(
                [jnp.concatenate([a_scr[pl.ds(n2 + j, FFT_K1P, stride=A_PITCH), :],
                                  a_scr[pl.ds(FFT_R + n2 + j, FFT_K1P, stride=A_PITCH), :]], axis=0)
                 for j in range(FFT_NG)], axis=1).astype(BF16)
            y = jnp.dot(f1i, bn, preferred_element_type=F32)
            for j in range(FFT_NG):
                u_scr[pl.ds(n2 + j, FFT_H, stride=U_PITCH), :] = y[:, j * LANES:(j + 1) * LANES]
            return c

        lax.fori_loop(0, FFT_R // FFT_NG, inv2, 0, unroll=8)

        cw, cb = cw_ref[...], cb_ref[...]

        def gate(n1, c):
            ub = pl.multiple_of(n1 * U_PITCH, 8)
            z = _conv3_chunk(gate_ref, n1, n_chunks, cw, cb) * u_scr[pl.ds(ub, FFT_R), :]
            if order == HY_ORDER - 1:
                o_ref[pl.ds(pl.multiple_of(n1 * FFT_R, FFT_R), FFT_R), :] = z.astype(o_ref.dtype)
            else:
                u_scr[pl.ds(ub, FFT_R), :] = z
            return c

        lax.fori_loop(0, n_chunks, gate, 0)


def _dft_tables():
    n = FFT_R * FFT_R
    k1 = np.arange(FFT_K1)[:, None]
    n1 = np.arange(FFT_H)[None, :]
    th = 2.0 * np.pi * k1 * n1 / FFT_R
    f1 = np.zeros((2 * FFT_K1P, FFT_H), np.float64)
    f1[:FFT_K1] = np.cos(th)
    f1[FFT_K1P:FFT_K1P + FFT_K1] = -np.sin(th)
    wgt = np.full((FFT_K1, 1), 2.0)
    wgt[0] = 1.0
    wgt[-1] = 1.0
    f1i = np.zeros((FFT_H, 2 * FFT_K1P), np.float64)
    f1i[:, :FFT_K1] = (wgt * np.cos(th)).T / n
    f1i[:, FFT_K1P:FFT_K1P + FFT_K1] = (-wgt * np.sin(th)).T / n
    kk = np.arange(FFT_K1)[:, None, None] + FFT_R * np.arange(FFT_R)[None, :, None]
    n2 = np.arange(FFT_R)[None, None, :]
    ph = 2.0 * np.pi * ((kk * n2) % n) / n
    gr, gi = np.cos(ph), -np.sin(ph)
    g = np.concatenate([np.concatenate([gr, -gi], axis=2), np.concatenate([gi, gr], axis=2)], axis=1)
    return (jnp.asarray(f1, F32).astype(BF16), jnp.asarray(g, F32).astype(BF16),
            jnp.asarray(f1i, F32).astype(BF16))


def _hyena_latent(p, conv_w, conv_b, kf):
    b, s, _ = p.shape
    assert s == FFT_H * FFT_R
    cb_ = LANES
    ncb = HY_W // cb_
    f1, g, f1i = _dft_tables()
    conv_b2 = conv_b.reshape(1, -1)
    seq = lambda off: pl.BlockSpec((None, s, cb_), lambda c, i: (i, 0, off + c))
    cws = lambda off: pl.BlockSpec((3, cb_), lambda c, i: (0, off + c))
    cbs = lambda off: pl.BlockSpec((1, cb_), lambda c, i: (0, off + c))
    return pl.pallas_call(
        _hyena_kernel,
        grid=(ncb, b),
        in_specs=[
            seq(0), seq(ncb), seq(2 * ncb),
            cws(0), cws(ncb), cws(2 * ncb),
            cbs(0), cbs(ncb), cbs(2 * ncb),
            _resident(f1.shape, lambda c, i: (0, 0)),
            _resident(g.shape, lambda c, i: (0, 0, 0)),
            _resident(f1i.shape, lambda c, i: (0, 0)),
            _resident((HY_ORDER, FFT_K1, 2 * FFT_R, cb_), lambda c, i: (0, 0, 0, c)),
        ],
        out_specs=pl.BlockSpec((None, s, cb_), lambda c, i: (i, 0, c)),
        out_shape=jax.ShapeDtypeStruct((b, s, HY_W), BF16),
        scratch_shapes=[
            pltpu.VMEM((FFT_H * U_PITCH, cb_), F32),
            pltpu.VMEM((FFT_K1P * A_PITCH, cb_), F32),
        ],
        compiler_params=_cparams(("arbitrary", "arbitrary")),
        name="hyena_latent",
    )(p, p, p, conv_w, conv_w, conv_w, conv_b2, conv_b2, conv_b2, f1, g, f1i, kf)


def _hyena_ctx_kernel(u_ref, cw_ref, cb_ref, fc_ref, fi_ref, kf_ref, o_ref):
    l = u_ref.shape[0]
    half = fc_ref.shape[0] // 2
    u = u_ref[...].astype(F32)
    row = lax.broadcasted_iota(jnp.int32, u.shape, 0)
    cw = cw_ref[...]
    um1 = jnp.where(row == 0, 0.0, pltpu.roll(u, 1, 0))
    up1 = jnp.where(row == l - 1, 0.0, pltpu.roll(u, l - 1, 0))
    uc = cw[0:1] * um1 + cw[1:2] * u + cw[2:3] * up1 + cb_ref[...]
    z = uc[:, :HY_W]
    hi = lax.Precision.HIGHEST
    for order in range(HY_ORDER):
        gate = uc[:, (order + 1) * HY_W:(order + 2) * HY_W]
        xf = jnp.dot(fc_ref[...], z, precision=hi, preferred_element_type=F32)
        kf = kf_ref[order]
        xr, xi = xf[:half], xf[half:]
        kr, ki = kf[:half], kf[half:]
        y = jnp.concatenate([xr * kr - xi * ki, xr * ki + xi * kr], axis=0)
        z = gate * jnp.dot(fi_ref[...], y, precision=hi, preferred_element_type=F32)
    o_ref[...] = z.astype(o_ref.dtype)


def _ctx_dft_tables(l):
    n = 2 * l
    half = l + 8
    k = np.arange(l + 1)[:, None]
    t = np.arange(l)[None, :]
    th = 2.0 * np.pi * ((k * t) % n) / n
    fc = np.zeros((2 * half, l), np.float64)
    fc[:l + 1] = np.cos(th)
    fc[half:half + l + 1] = -np.sin(th)
    wgt = np.full((l + 1, 1), 2.0)
    wgt[0] = 1.0
    wgt[-1] = 1.0
    fi = np.zeros((l, 2 * half), np.float64)
    fi[:, :l + 1] = (wgt * np.cos(th)).T / n
    fi[:, half:half + l + 1] = (-wgt * np.sin(th)).T / n
    return jnp.asarray(fc, F32), jnp.asarray(fi, F32)


def _hyena_ctx(p, conv_w, conv_b, kf, fc, fi):
    b, l, _ = p.shape
    half = fc.shape[0] // 2
    wtot = 3 * HY_W
    return pl.pallas_call(
        _hyena_ctx_kernel,
        grid=(b,),
        in_specs=[
            pl.BlockSpec((None, l, wtot), lambda i: (i, 0, 0)),
            pl.BlockSpec((3, wtot), lambda i: (0, 0)),
            pl.BlockSpec((1, wtot), lambda i: (0, 0)),
            pl.BlockSpec((2 * half, l), lambda i: (0, 0)),
            pl.BlockSpec((l, 2 * half), lambda i: (0, 0)),
            pl.BlockSpec((HY_ORDER, 2 * half, HY_W), lambda i: (0, 0, 0)),
        ],
        out_specs=pl.BlockSpec((None, l, HY_W), lambda i: (i, 0, 0)),
        out_shape=jax.ShapeDtypeStruct((b, l, HY_W), BF16),
        compiler_params=_cparams(("parallel",)),
        name="hyena_ctx",
    )(p, conv_w, conv_b.reshape(1, -1), fc, fi, kf)


def _mix_ffn_kernel(*refs, recurrent):
    if recurrent:
        (x_ref, z_ref, hf_ref, hr_ref, gx_ref, wo_ref, g1_ref, gain_ref, sc_ref, sh_ref, g2_ref,
         w1_ref, w3_ref, w2_ref, o_ref, x1_scr, h_scr, t_scr) = refs
    else:
        (x_ref, m_ref, wo_ref, g1_ref, gain_ref, sc_ref, sh_ref, g2_ref,
         w1_ref, w3_ref, w2_ref, o_ref, x1_scr, h_scr, t_scr) = refs
    if recurrent:
        hw = z_ref.shape[-1]
        rec = ((hf_ref[...] + hr_ref[...]) * jax.nn.gelu(gx_ref[...].astype(F32))).astype(BF16)
        mixed = (jnp.dot(z_ref[...], wo_ref[:hw, :], preferred_element_type=F32)
                 + jnp.dot(rec, wo_ref[hw:, :], preferred_element_type=F32))
    else:
        mixed = jnp.dot(m_ref[...], wo_ref[...], preferred_element_type=F32)
    x1 = x_ref[...] + g1_ref[...] * mixed
    x1_scr[...] = x1
    h_scr[...] = _norm_mod(x1, gain_ref[...], sc_ref[...], sh_ref[...]).astype(BF16)

    d, ff = w1_ref.shape
    for c0 in range(0, ff, FFN_HIDDEN_CHUNK):
        cols = slice(c0, min(c0 + FFN_HIDDEN_CHUNK, ff))
        h = h_scr[...]
        a = jnp.dot(h, w1_ref[:, cols], preferred_element_type=F32)
        bgate = jnp.dot(h, w3_ref[:, cols], preferred_element_type=F32)
        t_scr[:, cols] = (jax.nn.silu(a) * bgate).astype(BF16)
    for n0 in range(0, d, FFN_OUT_CHUNK):
        cols = slice(n0, n0 + FFN_OUT_CHUNK)
        y = jnp.dot(t_scr[...], w2_ref[:, cols], preferred_element_type=F32)
        o_ref[:, cols] = x1_scr[:, cols] + g2_ref[:, cols] * y


def _mix_ffn(x, mixer_inputs, wo, g1, gain, sc, sh, g2, w1, w3, w2, tm, recurrent):
    b, s, d = x.shape
    ff = w1.shape[1]
    row = lambda width, col=0: pl.BlockSpec((None, tm, width), lambda i, r: (i, r, col))
    per_b = pl.BlockSpec((None, 1, d), lambda i, r: (i, 0, 0))
    if recurrent:
        z, hf, hr, p = mixer_inputs
        mix_specs = [row(HY_W), row(LRU_W), row(LRU_W), row(LRU_W, (3 * HY_W + LRU_W) // LRU_W)]
        mix_args = [z, hf, hr, p]
    else:
        (m,) = mixer_inputs
        mix_specs = [row(d)]
        mix_args = [m]
    return pl.pallas_call(
        functools.partial(_mix_ffn_kernel, recurrent=recurrent),
        grid=(b, s // tm),
        in_specs=[row(d)] + mix_specs + [
            _resident((d, d), lambda i, r: (0, 0)),
            per_b,
            pl.BlockSpec((1, d), lambda i, r: (0, 0)),
            per_b, per_b, per_b,
            _resident((d, ff), lambda i, r: (0, 0)),
            _resident((d, ff), lambda i, r: (0, 0)),
            _resident((ff, d), lambda i, r: (0, 0)),
        ],
        out_specs=row(d),
        out_shape=jax.ShapeDtypeStruct((b, s, d), F32),
        scratch_shapes=[
            pltpu.VMEM((tm, d), F32),
            pltpu.VMEM((tm, d), BF16),
            pltpu.VMEM((tm, ff), BF16),
        ],
        compiler_params=_cparams(("parallel", "parallel")),
        name="mix_ffn_rec" if recurrent else "mix_ffn_attn",
    )(x, *mix_args, wo, g1, gain, sc, sh, g2, w1, w3, w2)


def _filter_features(l):
    t = jnp.linspace(0.0, 1.0, l, dtype=F32)[:, None]
    bands = jnp.linspace(1e-4, HY_BANDS - 1, HY_BANDS, dtype=F32)
    w = 2.0 * math.pi * jnp.arange(l, dtype=F32)[:, None] / l
    z = jnp.concatenate([t, jnp.cos(bands * w), -jnp.sin(bands * w)], axis=-1)
    return jnp.pad(z, ((0, 0), (0, FILT_EMB_PAD - z.shape[1])))


def _filter_mlp_kernel(z_ref, w1_ref, b1_ref, w2_ref, b2_ref, fr_ref, o_ref):
    hp = lax.Precision.HIGHEST
    fr = fr_ref[...]
    h = jnp.sin(fr * (jnp.dot(z_ref[...], w1_ref[...], precision=hp, preferred_element_type=F32) + b1_ref[...]))
    o_ref[...] = jnp.sin(fr * (jnp.dot(h, w2_ref[...], precision=hp, preferred_element_type=F32) + b2_ref[...]))


def _filter_mlp(l, w1, b1, w2, b2, freq):
    z = _filter_features(l)
    w1p = jnp.pad(w1, ((0, FILT_EMB_PAD - w1.shape[0]), (0, 0)))
    hid = w2.shape[0]
    return pl.pallas_call(
        _filter_mlp_kernel,
        out_shape=jax.ShapeDtypeStruct((l, hid), F32),
        compiler_params=pltpu.CompilerParams(vmem_limit_bytes=VMEM_LIMIT_BYTES),
        name="filter_mlp",
    )(z, w1p, b1[None], w2, b2[None], freq[None])


def _filter_decay():
    max_decay = math.log(HY_DECAY_TARGET) / HY_FAST_DECAY
    min_decay = math.log(HY_DECAY_TARGET) / HY_SLOW_DECAY
    return jnp.abs(jnp.linspace(min_decay, max_decay, HY_W, dtype=F32))[None]


def _filter_taps(h, t, w3f, w3b, decay, row0):
    hp = lax.Precision.HIGHEST
    window = jnp.exp(-t * decay) + HY_MOD_SHIFT
    taps = jnp.dot(h, jnp.concatenate([w3f, w3b], axis=1), precision=hp, preferred_element_type=F32)
    hf = taps[:, :LANES] * window
    hb = taps[:, LANES:] * window
    row = lax.broadcasted_iota(jnp.int32, hb.shape, 0) + row0
    return hf, jnp.where(row == 0, 0.0, hb)


def _filter_spec_latent_kernel(h_ref, t_ref, w3f_ref, w3b_ref, dec_ref, bias_ref, f1_ref, g_ref,
                               o_ref, uf_scr, ub_scr, af_scr, ab_scr):
    w3f, w3b, dec = w3f_ref[...], w3b_ref[...], dec_ref[...]

    def load(n1, acc):
        base = pl.multiple_of(n1 * FFT_R, FFT_R)
        hf, hb = _filter_taps(h_ref[pl.ds(base, FFT_R), :], t_ref[pl.ds(base, FFT_R), :], w3f, w3b, dec, base)
        ub = pl.multiple_of(n1 * U_PITCH, 8)
        uf_scr[pl.ds(ub, FFT_R), :] = hf
        ub_scr[pl.ds(ub, FFT_R), :] = hb
        return acc + jnp.sum(jnp.abs(hf) + jnp.abs(hb), axis=0, keepdims=True)

    l1 = lax.fori_loop(0, FFT_H, load, jnp.zeros((1, LANES), F32), unroll=4)
    inv = 1.0 / l1
    bias = bias_ref[...]
    f1 = f1_ref[...]
    _fwd_level1(uf_scr, af_scr, f1)
    _fwd_level1(ub_scr, ab_scr, f1)

    def level2(k1, c):
        base = pl.multiple_of(k1 * A_PITCH, 8)
        g = g_ref[k1]
        xf = jnp.dot(g, af_scr[pl.ds(base, 2 * FFT_R), :].astype(BF16), preferred_element_type=F32)
        xb = jnp.dot(g, ab_scr[pl.ds(base, 2 * FFT_R), :].astype(BF16), preferred_element_type=F32)
        kr = (xf[:FFT_R] + xb[:FFT_R]) * inv + bias
        ki = (xf[FFT_R:] - xb[FFT_R:]) * inv
        o_ref[k1] = jnp.concatenate([kr, ki], axis=0).astype(o_ref.dtype)
        return c

    lax.fori_loop(0, FFT_K1, level2, 0, unroll=5)


def _filter_spec_latent(h, w3, bias):
    l, hid = h.shape
    assert l == FFT_H * FFT_R
    f1, g, _ = _dft_tables()
    ncb = HY_W // LANES
    t = jnp.linspace(0.0, 1.0, l, dtype=F32)[:, None]
    return pl.pallas_call(
        _filter_spec_latent_kernel,
        grid=(HY_ORDER * ncb,),
        in_specs=[
            pl.BlockSpec((l, hid), lambda i: (0, 0)),
            pl.BlockSpec((l, 1), lambda i: (0, 0)),
            pl.BlockSpec((hid, LANES), lambda i: (0, i)),
            pl.BlockSpec((hid, LANES), lambda i: (0, HY_ORDER * ncb + i)),
            pl.BlockSpec((1, LANES), lambda i: (0, i % ncb)),
            pl.BlockSpec((None, 1, LANES), lambda i: (i // ncb, 0, i % ncb)),
            _resident(f1.shape, lambda i: (0, 0)),
            _resident(g.shape, lambda i: (0, 0, 0)),
        ],
        out_specs=pl.BlockSpec((None, FFT_K1, 2 * FFT_R, LANES), lambda i: (i // ncb, 0, 0, i % ncb)),
        out_shape=jax.ShapeDtypeStruct((HY_ORDER, FFT_K1, 2 * FFT_R, HY_W), BF16),
        scratch_shapes=[
            pltpu.VMEM((FFT_H * U_PITCH, LANES), F32),
            pltpu.VMEM((FFT_H * U_PITCH, LANES), F32),
            pltpu.VMEM((FFT_K1P * A_PITCH, LANES), F32),
            pltpu.VMEM((FFT_K1P * A_PITCH, LANES), F32),
        ],
        compiler_params=_cparams(("arbitrary",)),
        name="filter_spec_latent",
    )(h, t, w3, w3, _filter_decay(), bias[:, None, :], f1, g)


def _filter_spec_ctx_kernel(h_ref, t_ref, w3f_ref, w3b_ref, dec_ref, bias_ref, fc_ref, o_ref):
    hp = lax.Precision.HIGHEST
    half = fc_ref.shape[0] // 2
    hf, hb = _filter_taps(h_ref[...], t_ref[...], w3f_ref[...], w3b_ref[...], dec_ref[...], 0)
    inv = 1.0 / jnp.sum(jnp.abs(hf) + jnp.abs(hb), axis=0, keepdims=True)
    xf = jnp.dot(fc_ref[...], hf, precision=hp, preferred_element_type=F32)
    xb = jnp.dot(fc_ref[...], hb, precision=hp, preferred_element_type=F32)
    kr = (xf[:half] + xb[:half]) * inv + bias_ref[...]
    ki = (xf[half:] - xb[half:]) * inv
    o_ref[...] = jnp.concatenate([kr, ki], axis=0)


def _filter_spec_ctx(h, w3, bias, fc):
    l, hid = h.shape
    ncb = HY_W // LANES
    t = jnp.linspace(0.0, 1.0, l, dtype=F32)[:, None]
    return pl.pallas_call(
        _filter_spec_ctx_kernel,
        grid=(HY_ORDER * ncb,),
        in_specs=[
            pl.BlockSpec((l, hid), lambda i: (0, 0)),
            pl.BlockSpec((l, 1), lambda i: (0, 0)),
            pl.BlockSpec((hid, LANES), lambda i: (0, i)),
            pl.BlockSpec((hid, LANES), lambda i: (0, HY_ORDER * ncb + i)),
            pl.BlockSpec((1, LANES), lambda i: (0, i % ncb)),
            pl.BlockSpec((None, 1, LANES), lambda i: (i // ncb, 0, i % ncb)),
            pl.BlockSpec(fc.shape, lambda i: (0, 0)),
        ],
        out_specs=pl.BlockSpec((None, fc.shape[0], LANES), lambda i: (i // ncb, 0, i % ncb)),
        out_shape=jax.ShapeDtypeStruct((HY_ORDER, fc.shape[0], HY_W), F32),
        compiler_params=_cparams(("parallel",)),
        name="filter_spec_ctx",
    )(h, t, w3, w3, _filter_decay(), bias[:, None, :], fc)


def _mod_kernel(c_ref, w_ref, b_ref, o_ref):
    act = jax.nn.silu(c_ref[...])
    o_ref[...] = jnp.dot(act, w_ref[...], precision=lax.Precision.HIGHEST,
                         preferred_element_type=F32) + b_ref[...]


def _modulations(c, c_ctx, w_mod, b_mod):
    depth, d, n = w_mod.shape
    nb = c.shape[0]
    rows = 2 * nb
    cc = jnp.zeros((rows, d), F32).at[:nb].set(c).at[nb].set(c_ctx)
    tn = n // 4
    return pl.pallas_call(
        _mod_kernel,
        grid=(depth, n // tn),
        in_specs=[
            pl.BlockSpec((rows, d), lambda i, j: (0, 0)),
            pl.BlockSpec((None, d, tn), lambda i, j: (i, 0, j)),
            pl.BlockSpec((None, 1, tn), lambda i, j: (i, 0, j)),
        ],
        out_specs=pl.BlockSpec((None, rows, tn), lambda i, j: (i, 0, j)),
        out_shape=jax.ShapeDtypeStruct((depth, rows, n), F32),
        compiler_params=_cparams(("parallel", "parallel")),
        name="adaln_mod",
    )(cc, w_mod, b_mod[:, None, :])


def _block_diag(w):
    h, bw, _ = w.shape
    eye = jnp.eye(h, dtype=w.dtype)
    return (eye[:, None, :, None] * w[:, :, None, :]).reshape(h * bw, h * bw)


def _dup_kv_columns(w):
    nq = N_HEADS * HEAD_DIM
    d = w.shape[0]
    kv = w[:, nq:].reshape(d, 2 * N_KV, 1, HEAD_DIM)
    kv = jnp.broadcast_to(kv, (d, 2 * N_KV, LANES // HEAD_DIM, HEAD_DIM)).reshape(d, -1)
    return jnp.concatenate([w[:, :nq], kv], axis=1)


def _split_mod(m):
    return [t[:, None, :] for t in jnp.split(m, 6, axis=-1)]


def _rope_tables(s):
    half = HEAD_DIM // 2
    nf = half // 2
    inv = jnp.power(ROPE_BASE, -jnp.arange(nf, dtype=F32) / nf)
    pos = jnp.arange(s, dtype=jnp.int32)
    row = (pos // GRID_W).astype(F32)[:, None] * inv
    col = (pos % GRID_W).astype(F32)[:, None] * inv
    cos = jnp.concatenate([jnp.cos(row), jnp.cos(row), jnp.cos(col), jnp.cos(col)], axis=-1)
    sin = jnp.concatenate([-jnp.sin(row), jnp.sin(row), -jnp.sin(col), jnp.sin(col)], axis=-1)
    return jnp.tile(cos, (1, LANES // HEAD_DIM)), jnp.tile(sin, (1, LANES // HEAD_DIM))


def _rope_gain_tables(cos, sin, gain, scale):
    g = jnp.tile(gain, LANES // HEAD_DIM)
    first_half = (jnp.arange(LANES) % 32) < 16
    g_partner = jnp.where(first_half, jnp.roll(g, -16), jnp.roll(g, 16))
    return cos * (g * scale)[None], sin * (g_partner * scale)[None]


def kernel(x, c, ctx, c_ctx, norm1, norm2, w_mod, b_mod, ffn_w1, ffn_w3, ffn_w2, ab_w_in, hy_conv_w, hy_conv_b, hy_f_w1, hy_f_b1, hy_f_w2, hy_f_b2, hy_f_w3, hy_f_freq, hy_bias, lru_conv_w, lru_conv_b, lru_w_a, lru_b_a, lru_w_i, lru_b_i, lru_lam, ab_w_out, at_w_qkv, at_q_gain, at_k_gain, at_sink, at_w_o):
    nb, s, d = x.shape
    cl = ctx.shape[1]
    bf = lambda a: a.astype(BF16)

    mods = _modulations(c, c_ctx, w_mod, b_mod)
    sh1x, sc1x, g1x, sh2x, sc2x, g2x = _split_mod(mods[0, :nb])
    sh1c, sc1c, g1c, sh2c, sc2c, g2c = _split_mod(jnp.broadcast_to(mods[0, nb:nb + 1], (nb, mods.shape[-1])))
    gain1 = norm1[0][None]
    gain2 = norm2[0][None]
    w_in = bf(ab_w_in[0])
    tm_x, tm_c = min(ROW_TILE, s), min(ROW_TILE, cl)
    px = _proj(x, gain1, sc1x, sh1x, w_in, tm_x)
    pc = _proj(ctx, gain1, sc1c, sh1c, w_in, tm_c)

    mlp = (hy_f_w1[0], hy_f_b1[0], hy_f_w2[0], hy_f_b2[0], hy_f_freq[0])
    fc, fi = _ctx_dft_tables(cl)
    kf_x = _filter_spec_latent(_filter_mlp(s, *mlp), hy_f_w3[0], hy_bias[0])
    kf_c = _filter_spec_ctx(_filter_mlp(cl, *mlp), hy_f_w3[0], hy_bias[0], fc)
    zx = _hyena_latent(px, hy_conv_w[0], hy_conv_b[0], kf_x)
    zc = _hyena_ctx(pc, hy_conv_w[0], hy_conv_b[0], kf_c, fc, fi)

    lru_col = 3 * HY_W // LRU_W
    lcb = lru_conv_b[0][None]
    h0 = jnp.zeros((nb, LRU_W), F32)
    hx, hc = [], []
    for dr, reverse in enumerate((False, True)):
        wg = bf(0.5 * jnp.concatenate([_block_diag(lru_w_a[0, dr]), _block_diag(lru_w_i[0, dr])], axis=1))
        bg = 0.5 * jnp.concatenate([lru_b_a[0, dr], lru_b_i[0, dr]])[None]
        lam = lru_lam[0, dr][None]
        hcs, h_end = _lru(pc, lru_col, lru_conv_w[0], lcb, wg, bg, lam, h0, reverse, min(LRU_TIME_CHUNK, cl))
        hxs, _ = _lru(px, lru_col, lru_conv_w[0], lcb, wg, bg, lam, h_end, reverse, min(LRU_TIME_CHUNK, s))
        hx.append(hxs)
        hc.append(hcs)

    w_out = bf(ab_w_out[0])
    w1, w3, w2 = bf(ffn_w1[0]), bf(ffn_w3[0]), bf(ffn_w2[0])
    x = _mix_ffn(x, (zx, hx[0], hx[1], px), w_out, g1x, gain2, sc2x, sh2x, g2x, w1, w3, w2, tm_x, True)
    ctx = _mix_ffn(ctx, (zc, hc[0], hc[1], pc), w_out, g1c, gain2, sc2c, sh2c, g2c, w1, w3, w2, tm_c, True)

    sh1x, sc1x, g1x, sh2x, sc2x, g2x = _split_mod(mods[1, :nb])
    sh1c, sc1c = _split_mod(jnp.broadcast_to(mods[1, nb:nb + 1], (nb, mods.shape[-1])))[:2]
    gain1 = norm1[1][None]
    gain2 = norm2[1][None]
    w_qkv = bf(_dup_kv_columns(at_w_qkv[0]))
    cos, sin = _rope_tables(s)
    q_scale = HEAD_DIM ** -0.5
    q, k, v = _qkv(x, gain1, sc1x, sh1x, w_qkv,
                   *_rope_gain_tables(cos, sin, at_q_gain[0], q_scale),
                   *_rope_gain_tables(cos, sin, at_k_gain[0], 1.0), tm_x)
    no_rot = (jnp.ones((cl, LANES), F32), jnp.zeros((cl, LANES), F32))
    _, kx, vx = _qkv(ctx, gain1, sc1c, sh1c, w_qkv,
                     *_rope_gain_tables(*no_rot, at_q_gain[0], q_scale),
                     *_rope_gain_tables(*no_rot, at_k_gain[0], 1.0), tm_c)
    o = _attention(q, k, v, kx, vx, at_sink[0], WINDOW)
    w1, w3, w2 = bf(ffn_w1[1]), bf(ffn_w3[1]), bf(ffn_w2[1])
    return _mix_ffn(x, (o,), bf(at_w_o[0]), g1x, gain2, sc2x, sh2x, g2x, w1, w3, w2, tm_x, False)
```
